```python
import jax, jax.numpy as jnp
from jax import lax
import numpy as np

D_MODEL = 2048
BATCH = 4
SEQ = 4096
DEPTH = 2

N_A_LAYERS = max(1, DEPTH // 2)
N_B_LAYERS = DEPTH - N_A_LAYERS

HGRN_EXPAND = 128
HGRN_HEADS = D_MODEL // HGRN_EXPAND
HGRN_HEAD_V = D_MODEL // HGRN_HEADS
HGRN_CHUNK = 32

HEAD_DIM = 64
N_Q_HEADS = D_MODEL // HEAD_DIM
N_KV_HEADS = N_Q_HEADS // 8
GROUP = N_Q_HEADS // N_KV_HEADS
WINDOW = 128
ROT_DIM = HEAD_DIM // 4
ROPE_THETA = 500000.0

D_FF = 5504
N_SUBLAYERS = 3
NORM_EPS = 1e-6
NEG_INF = -1e30

kernel_name = "yoco_hgrn2_swa_sink_macaron_adaln"


def rmsnorm(x, gain):
    xf = x.astype(jnp.float32)
    y = xf * lax.rsqrt(jnp.mean(xf * xf, axis=-1, keepdims=True) + NORM_EPS)
    return (y * gain.astype(jnp.float32)).astype(x.dtype)


def ada_norm(h, gain, shift, scale):
    return rmsnorm(h, gain) * (1.0 + scale[:, None, :]) + shift[:, None, :]


def swiglu(u, w_in, w_out):
    a, b = jnp.split(u @ w_in, 2, axis=-1)
    return (jax.nn.silu(a) * b) @ w_out


def rope_tables(seq):
    inv_freq = jnp.power(jnp.float32(ROPE_THETA), -jnp.arange(0, ROT_DIM, 2, dtype=jnp.float32) / ROT_DIM)
    ang = jnp.arange(seq, dtype=jnp.float32)[:, None] * inv_freq[None, :]
    return jnp.sin(ang), jnp.cos(ang)


def partial_rotary(t, sin, cos):
    half = ROT_DIM // 2
    s = sin[None, :, None, :].astype(t.dtype)
    c = cos[None, :, None, :].astype(t.dtype)
    t1, t2, rest = t[..., :half], t[..., half:ROT_DIM], t[..., ROT_DIM:]
    return jnp.concatenate([t1 * c - t2 * s, t2 * c + t1 * s, rest], axis=-1)


def hgrn2_chunk_scan(q, k, v, log_f):
    B, S, H, K = q.shape
    V = v.shape[-1]
    nc = S // HGRN_CHUNK

    def to_chunks(t):
        return t.reshape(B, nc, HGRN_CHUNK, H, t.shape[-1]).transpose(1, 0, 3, 2, 4)

    causal = jnp.tril(jnp.ones((HGRN_CHUNK, HGRN_CHUNK), dtype=bool))

    def step(state, inp):
        qc, kc, vc, gc = inp
        b = jnp.cumsum(gc, axis=2)
        inter = jnp.einsum('bhck,bhkv->bhcv', qc * jnp.exp(b), state)
        rel = b[:, :, :, None, :] - b[:, :, None, :, :]
        decay = jnp.exp(jnp.where(causal[None, None, :, :, None], rel, -jnp.inf))
        scores = jnp.einsum('bhtk,bhsk,bhtsk->bhts', qc, kc, decay)
        out = inter + jnp.einsum('bhts,bhsv->bhtv', scores, vc)
        b_last = b[:, :, -1, :]
        k_dec = kc * jnp.exp(b_last[:, :, None, :] - b)
        new_state = jnp.exp(b_last)[..., None] * state + jnp.einsum('bhck,bhcv->bhkv', k_dec, vc)
        return new_state, out

    state0 = jnp.zeros((B, H, K, V), jnp.float32)
    _, outs = lax.scan(step, state0, (to_chunks(q), to_chunks(k), to_chunks(v), to_chunks(log_f)))
    return outs.transpose(1, 0, 3, 2, 4).reshape(B, S, H, V)


def hgrn2_mixer(u, w_in, lower_bound, head_gain, w_out):
    B, S, _ = u.shape
    q, f, i, g = jnp.split(u @ w_in, 4, axis=-1)
    forget = lower_bound + (1.0 - lower_bound) * jax.nn.sigmoid(f.astype(jnp.float32))

    def heads(t):
        return t.reshape(B, S, HGRN_HEADS, -1)

    o = hgrn2_chunk_scan(heads(jax.nn.silu(q).astype(jnp.float32)),
                         heads(1.0 - forget),
                         heads(i.astype(jnp.float32)),
                         heads(jnp.log(forget)))
    o = rmsnorm(o, head_gain.reshape(HGRN_HEADS, HGRN_HEAD_V)).astype(u.dtype)
    o = o.reshape(B, S, D_MODEL) * jax.nn.sigmoid(g)
    return o @ w_out


def shared_kv(h, cs, kv_gain, w_ada_kv, b_ada_kv, w_kv, b_kv, sin, cos):
    B, S, _ = h.shape
    shift, scale = jnp.split(cs @ w_ada_kv + b_ada_kv, 2, axis=-1)
    u = ada_norm(h, kv_gain, shift, scale)
    k, v = jnp.split(u @ w_kv + b_kv, 2, axis=-1)
    k = partial_rotary(k.reshape(B, S, N_KV_HEADS, HEAD_DIM), sin, cos)
    v = v.reshape(B, S, N_KV_HEADS, HEAD_DIM)
    return k, v


def sliding_window_attention(q, k, v, sinks):
    B, S, _, Dh = q.shape
    nb = S // WINDOW
    qb = q.reshape(B, nb, WINDOW, N_KV_HEADS, GROUP, Dh)

    def band(t):
        tb = t.reshape(B, nb, WINDOW, N_KV_HEADS, Dh)
        prev = jnp.pad(tb, ((0, 0), (1, 0), (0, 0), (0, 0), (0, 0)))[:, :nb]
        return jnp.concatenate([prev, tb], axis=2)

    kb, vb = band(k), band(v)
    scores = jnp.einsum('bnqhgd,bnkhd->bnhgqk', qb, kb).astype(jnp.float32) * (Dh ** -0.5)
    qi = jnp.arange(WINDOW)[:, None]
    kj = jnp.arange(2 * WINDOW)[None, :]
    in_window = (kj > qi) & (kj <= qi + WINDOW)
    not_before_start = (jnp.arange(nb)[:, None, None] > 0) | (kj[None] >= WINDOW)
    mask = in_window[None] & not_before_start
    scores = jnp.where(mask[None, :, None, None], scores, NEG_INF)
    sink = sinks.astype(jnp.float32).reshape(N_KV_HEADS, GROUP)[None, None, :, :, None, None]
    m = jnp.maximum(jnp.max(scores, axis=-1, keepdims=True), sink)
    p = jnp.exp(scores - m)
    probs = p / (jnp.sum(p, axis=-1, keepdims=True) + jnp.exp(sink - m))
    out = jnp.einsum('bnhgqk,bnkhd->bnqhgd', probs.astype(v.dtype), vb)
    return out.reshape(B, S, N_Q_HEADS * Dh)


def setup_inputs(seed: int = 0) -> dict:
    key = jax.random.key(seed)
    ks = jax.random.split(key, 22)
    f32 = jnp.float32

    def w(k, shape, fan_in, scale=1.0):
        return jax.random.normal(k, shape, f32) * (scale * fan_in ** -0.5)

    def gain(k, shape):
        return 1.0 + 0.02 * jax.random.normal(k, shape, f32)

    qdim = N_Q_HEADS * HEAD_DIM
    kvdim = 2 * N_KV_HEADS * HEAD_DIM
    return {
        "x": jax.random.normal(ks[0], (BATCH, SEQ, D_MODEL), f32),
        "c": jax.random.normal(ks[1], (BATCH, D_MODEL), f32),
        "norm_gain": gain(ks[2], (DEPTH, N_SUBLAYERS, D_MODEL)),
        "w_ada": w(ks[3], (DEPTH, D_MODEL, N_SUBLAYERS * 3 * D_MODEL), D_MODEL, 0.5),
        "b_ada": 0.02 * jax.random.normal(ks[4], (DEPTH, N_SUBLAYERS * 3 * D_MODEL), f32),
        "w_ffn_in": w(ks[5], (DEPTH, 2, D_MODEL, 2 * D_FF), D_MODEL),
        "w_ffn_out": w(ks[6], (DEPTH, 2, D_FF, D_MODEL), D_FF),
        "w_hgrn_in": w(ks[7], (N_A_LAYERS, D_MODEL, 4 * D_MODEL), D_MODEL),
        "hgrn_lb_logits": 0.5 * jax.random.normal(ks[8], (N_A_LAYERS + 1, D_MODEL), f32),
        "hgrn_head_gain": gain(ks[9], (N_A_LAYERS, D_MODEL)),
        "w_hgrn_out": w(ks[10], (N_A_LAYERS, D_MODEL, D_MODEL), D_MODEL),
        "kv_gain": gain(ks[11], (D_MODEL,)),
        "w_ada_kv": w(ks[12], (D_MODEL, 2 * D_MODEL), D_MODEL, 0.5),
        "b_ada_kv": 0.02 * jax.random.normal(ks[13], (2 * D_MODEL,), f32),
        "w_kv": w(ks[14], (D_MODEL, kvdim), D_MODEL),
        "b_kv": 0.02 * jax.random.normal(ks[15], (kvdim,), f32),
        "w_q": w(ks[16], (N_B_LAYERS, D_MODEL, qdim), D_MODEL),
        "b_q": 0.02 * jax.random.normal(ks[17], (N_B_LAYERS, qdim), f32),
        "attn_sinks": jax.random.normal(ks[18], (N_B_LAYERS, N_Q_HEADS), f32),
        "w_attn_out": w(ks[19], (N_B_LAYERS, qdim, D_MODEL), qdim),
        "final_gain": gain(ks[20], (D_MODEL,)),
    }


def reference(x, c, norm_gain, w_ada, b_ada, w_ffn_in, w_ffn_out, w_hgrn_in, hgrn_lb_logits,
              hgrn_head_gain, w_hgrn_out, kv_gain, w_ada_kv, b_ada_kv, w_kv, b_kv, w_q, b_q,
              attn_sinks, w_attn_out, final_gain):
    B, S, D = x.shape
    sin, cos = rope_tables(S)
    lb_all = jnp.cumsum(jax.nn.softmax(hgrn_lb_logits.astype(jnp.float32), axis=0), axis=0)
    cs = jax.nn.silu(c)
    h = x
    k_sh, v_sh = None, None
    for layer in range(DEPTH):
        mod = (cs @ w_ada[layer] + b_ada[layer]).reshape(B, N_SUBLAYERS, 3, D)

        u = ada_norm(h, norm_gain[layer, 0], mod[:, 0, 0], mod[:, 0, 1])
        h = h + 0.5 * mod[:, 0, 2][:, None, :] * swiglu(u, w_ffn_in[layer, 0], w_ffn_out[layer, 0])

        u = ada_norm(h, norm_gain[layer, 1], mod[:, 1, 0], mod[:, 1, 1])
        if layer < N_A_LAYERS:
            y = hgrn2_mixer(u, w_hgrn_in[layer], lb_all[layer], hgrn_head_gain[layer], w_hgrn_out[layer])
        else:
            bl = layer - N_A_LAYERS
            q = (u @ w_q[bl] + b_q[bl]).reshape(B, S, N_Q_HEADS, HEAD_DIM)
            q = partial_rotary(q, sin, cos)
            y = sliding_window_attention(q, k_sh, v_sh, attn_sinks[bl]) @ w_attn_out[bl]
        h = h + mod[:, 1, 2][:, None, :] * y

        u = ada_norm(h, norm_gain[layer, 2], mod[:, 2, 0], mod[:, 2, 1])
        h = h + 0.5 * mod[:, 2, 2][:, None, :] * swiglu(u, w_ffn_in[layer, 1], w_ffn_out[layer, 1])

        if layer == N_A_LAYERS - 1:
            k_sh, v_sh = shared_kv(h, cs, kv_gain, w_ada_kv, b_ada_kv, w_kv, b_kv, sin, cos)

    return rmsnorm(h, final_gain)
```

```python
import functools

import numpy as np
import jax
import jax.numpy as jnp
from jax import lax
from jax.experimental import pallas as pl
from jax.experimental.pallas import tpu as pltpu

F32 = jnp.float32
BF16 = jnp.bfloat16

NORM_EPS = 1e-6
NEG_INF = -1e30
ROPE_THETA = 500000.0

LANES = 128
SUBLANES = 8
MIB = 1 << 20

HGRN_HEAD = 128
HGRN_CHUNK = 128
HEAD_DIM = 64
ROT_DIM = HEAD_DIM // 4
GROUP = 8
WINDOW = 128
N_SUBLAYERS = 3


def _params(semantics, vmem_mib):
    return pltpu.CompilerParams(dimension_semantics=semantics, vmem_limit_bytes=vmem_mib * MIB)


def _sigmoid(x):
    return 1.0 / (1.0 + jnp.exp(-x))


def _ada_norm(h, gain, shift, scale):
    y = h * lax.rsqrt(jnp.mean(h * h, axis=-1, keepdims=True) + NORM_EPS)
    return (y * gain) * (1.0 + scale) + shift


def _mod_kernel(c_ref, w_ref, b_ref, o_ref):
    c = c_ref[...]
    cs = (c * _sigmoid(c)).astype(BF16)
    o_ref[...] = jnp.dot(cs, w_ref[...].astype(BF16), preferred_element_type=F32) + b_ref[...]


def _ada_mod(c_rows, w, b):
    L, D, N = w.shape
    tn = 1024
    return pl.pallas_call(
        _mod_kernel,
        grid=(L, N // tn),
        in_specs=[
            pl.BlockSpec((SUBLANES, D), lambda l, j: (0, 0)),
            pl.BlockSpec((None, D, tn), lambda l, j: (l, 0, j)),
            pl.BlockSpec((None, 1, tn), lambda l, j: (l, 0, j)),
        ],
        out_specs=pl.BlockSpec((None, SUBLANES, tn), lambda l, j: (l, 0, j)),
        out_shape=jax.ShapeDtypeStruct((L, SUBLANES, N), F32),
        compiler_params=_params(("arbitrary", "arbitrary"), 40),
        name="ada_mod",
    )(c_rows, w, b.reshape(L, 1, N))


def _ffn_kernel(*refs, n_f, final_norm):
    if final_norm:
        (h_ref, gain_ref, shift_ref, scale_ref, gate_ref, wa_ref, wb_ref, wo_ref, fg_ref,
         o_ref, u_scr, acc_scr) = refs
    else:
        (h_ref, gain_ref, shift_ref, scale_ref, gate_ref, wa_ref, wb_ref, wo_ref,
         o_ref, u_scr, acc_scr) = refs
    j = pl.program_id(1)

    @pl.when(j == 0)
    def _():
        u_scr[...] = _ada_norm(h_ref[...], gain_ref[...], shift_ref[...], scale_ref[...]).astype(BF16)
        acc_scr[...] = jnp.zeros_like(acc_scr)

    u = u_scr[...]
    a = jnp.dot(u, wa_ref[...], preferred_element_type=F32)
    b = jnp.dot(u, wb_ref[...], preferred_element_type=F32)
    g = ((a * _sigmoid(a)) * b).astype(BF16)
    acc_scr[...] += jnp.dot(g, wo_ref[...], preferred_element_type=F32)

    @pl.when(j == n_f - 1)
    def _():
        out = h_ref[...] + (0.5 * gate_ref[...]) * acc_scr[...]
        if final_norm:
            out = out * lax.rsqrt(jnp.mean(out * out, axis=-1, keepdims=True) + NORM_EPS) * fg_ref[...]
        o_ref[...] = out


def _ffn(h, gain, shift, scale, gate, w_in_p, w_out_p, seq, final_gain=None, tm=512, tf=512):
    T, D = h.shape
    Fp = w_out_p.shape[0]
    n_f = Fp // tf
    per_b = seq // tm
    final_norm = final_gain is not None
    vec = pl.BlockSpec((1, D), lambda i, j: (0, 0))
    bvec = pl.BlockSpec((None, 1, D), lambda i, j: (i // per_b, 0, 0))
    in_specs = [
        pl.BlockSpec((tm, D), lambda i, j: (i, 0)),
        vec, bvec, bvec, bvec,
        pl.BlockSpec((D, tf), lambda i, j: (0, j)),
        pl.BlockSpec((D, tf), lambda i, j: (0, n_f + j)),
        pl.BlockSpec((tf, D), lambda i, j: (j, 0)),
    ]
    args = [h, gain, shift, scale, gate, w_in_p, w_in_p, w_out_p]
    if final_norm:
        in_specs.append(vec)
        args.append(final_gain)
    return pl.pallas_call(
        functools.partial(_ffn_kernel, n_f=n_f, final_norm=final_norm),
        grid=(T // tm, n_f),
        in_specs=in_specs,
        out_specs=pl.BlockSpec((tm, D), lambda i, j: (i, 0)),
        out_shape=jax.ShapeDtypeStruct((T, D), F32),
        scratch_shapes=[pltpu.VMEM((tm, D), BF16), pltpu.VMEM((tm, D), F32)],
        compiler_params=_params(("arbitrary", "arbitrary"), 56),
        name="ffn",
    )(*args)


def _rotate_block(z, cos, sin_lo, sin_hi):
    half = ROT_DIM // 2
    up = pltpu.roll(z, LANES - half, axis=1)
    down = pltpu.roll(z, half, axis=1)
    return z * cos + up * sin_lo + down * sin_hi


def _norm_proj_kernel(*refs, has_bias, rot_cols):
    refs = list(refs)
    h_ref, gain_ref, shift_ref, scale_ref, w_ref = refs[:5]
    pos = 5
    bias_ref = None
    if has_bias:
        bias_ref = refs[pos]
        pos += 1
    if rot_cols:
        cos_ref, slo_ref, shi_ref = refs[pos:pos + 3]
        pos += 3
    o_ref, u_scr = refs[pos], refs[pos + 1]

    @pl.when(pl.program_id(1) == 0)
    def _():
        u_scr[...] = _ada_norm(h_ref[...], gain_ref[...], shift_ref[...], scale_ref[...]).astype(BF16)

    z = jnp.dot(u_scr[...], w_ref[...], preferred_element_type=F32)
    if has_bias:
        z = z + bias_ref[...]
    if rot_cols:
        cos, slo, shi = cos_ref[...], slo_ref[...], shi_ref[...]
        tn = z.shape[1]
        blocks = []
        for cb in range(tn // LANES):
            zb = z[:, cb * LANES:(cb + 1) * LANES]
            if cb * LANES < rot_cols:
                zb = _rotate_block(zb, cos, slo, shi)
            blocks.append(zb)
        z = jnp.concatenate(blocks, axis=1) if len(blocks) > 1 else blocks[0]
    o_ref[...] = z.astype(o_ref.dtype)


def _norm_proj(h, gain, shift, scale, w, seq, out_dtype, bias=None, rope=None, rot_cols=0, tm=512, tn=None):
    T, D = h.shape
    N = w.shape[1]
    tn = N if tn is None else tn
    per_b = seq // tm
    vec = pl.BlockSpec((1, D), lambda i, j: (0, 0))
    bvec = pl.BlockSpec((None, 1, D), lambda i, j: (i // per_b, 0, 0))
    in_specs = [pl.BlockSpec((tm, D), lambda i, j: (i, 0)), vec, bvec, bvec,
                pl.BlockSpec((D, tn), lambda i, j: (0, j))]
    args = [h, gain, shift, scale, w]
    if bias is not None:
        in_specs.append(pl.BlockSpec((1, tn), lambda i, j: (0, j)))
        args.append(bias)
    if rot_cols:
        assert tn == N and rot_cols % LANES == 0
        tab = pl.BlockSpec((tm, LANES), lambda i, j: (i % per_b, 0))
        in_specs += [tab, tab, tab]
        args += list(rope)
    return pl.pallas_call(
        functools.partial(_norm_proj_kernel, has_bias=bias is not None, rot_cols=rot_cols),
        grid=(T // tm, N // tn),
        in_specs=in_specs,
        out_specs=pl.BlockSpec((tm, tn), lambda i, j: (i, j)),
        out_shape=jax.ShapeDtypeStruct((T, N), out_dtype),
        scratch_shapes=[pltpu.VMEM((tm, D), BF16)],
        compiler_params=_params(("arbitrary", "arbitrary"), 48),
        name="norm_proj",
    )(*args)


def _proj_res_kernel(x_ref, w_ref, h_ref, gate_ref, o_ref):
    y = jnp.dot(x_ref[...], w_ref[...], preferred_element_type=F32)
    o_ref[...] = h_ref[...] + gate_ref[...] * y


def _proj_res(x, w, h, gate, seq, tm=512):
    T, K = x.shape
    N = w.shape[1]
    per_b = seq // tm
    return pl.pallas_call(
        _proj_res_kernel,
        grid=(T // tm,),
        in_specs=[
            pl.BlockSpec((tm, K), lambda i: (i, 0)),
            pl.BlockSpec((K, N), lambda i: (0, 0)),
            pl.BlockSpec((tm, N), lambda i: (i, 0)),
            pl.BlockSpec((None, 1, N), lambda i: (i // per_b, 0, 0)),
        ],
        out_specs=pl.BlockSpec((tm, N), lambda i: (i, 0)),
        out_shape=jax.ShapeDtypeStruct((T, N), F32),
        compiler_params=_params(("arbitrary",), 48),
        name="proj_res",
    )(x, w, h, gate)


def _hgrn_scan_kernel(zq_ref, zf_ref, zi_ref, zg_ref, lbl_ref, hg_ref, o_ref,
                      st_ref, b_scr, k_scr, v_scr, *, layer, n_chunks):
    C = HGRN_CHUNK
    K = HGRN_HEAD

    @pl.when(pl.program_id(2) == 0)
    def _():
        st_ref[...] = jnp.zeros_like(st_ref)

    logits = lbl_ref[...]
    e = jnp.exp(logits - jnp.max(logits, axis=0, keepdims=True))
    lb = jnp.sum(e[:layer + 1], axis=0, keepdims=True) / jnp.sum(e, axis=0, keepdims=True)
    head_gain = hg_ref[...]

    row_c = lax.broadcasted_iota(jnp.int32, (C, C), 0)
    col_c = lax.broadcasted_iota(jnp.int32, (C, C), 1)
    tril = (row_c >= col_c).astype(F32)
    row = lax.broadcasted_iota(jnp.int32, (C, K), 0)
    row8 = lax.broadcasted_iota(jnp.int32, (SUBLANES, K), 0)
    ones = jnp.ones((K, K), BF16)
    levels = []
    m = SUBLANES
    while m < C:
        levels.append(m)
        m *= 2

    def chunk(ci, carry):
        r0 = pl.multiple_of(ci * C, C)
        zq = zq_ref[pl.ds(r0, C), :]
        q = zq * _sigmoid(zq)
        forget = lb + (1.0 - lb) * _sigmoid(zf_ref[pl.ds(r0, C), :])
        kk = 1.0 - forget
        v = zi_ref[pl.ds(r0, C), :]
        b = jnp.dot(tril, jnp.log(forget), preferred_element_type=F32,
                    precision=lax.Precision.HIGHEST)
        b_scr[...] = b
        k_scr[...] = kk
        v_scr[...] = v
        v16 = v.astype(BF16)
        st = st_ref[...]

        qd = (q * jnp.exp(b)).astype(BF16)
        out = lax.dot_general(qd, st.astype(BF16), (((1,), (1,)), ((), ())), preferred_element_type=F32)
        b_last = b_scr[pl.ds(C - 1, 1), :]
        kd = (kk * jnp.exp(b_last - b)).astype(BF16)
        st_ref[...] = st * jnp.exp(b_last) + lax.dot_general(
            v16, kd, (((0,), (0,)), ((), ())), preferred_element_type=F32)

        p_off = None
        for m in levels:
            anchors = []
            for pb in range(C // (2 * m)):
                a_row = b_scr[pl.ds(pb * 2 * m + m - 1, 1), :]
                anchors.append(jnp.broadcast_to(a_row, (2 * m, K)))
            ba = jnp.concatenate(anchors, axis=0) if len(anchors) > 1 else anchors[0]
            later = (row & m) != 0
            ex = jnp.exp(jnp.where(later, b - ba, ba - b))
            qt = jnp.where(later, q * ex, 0.0).astype(BF16)
            kt = jnp.where(later, 0.0, kk * ex).astype(BF16)
            a_m = lax.dot_general(qt, kt, (((1,), (1,)), ((), ())), preferred_element_type=F32)
            if 2 * m < C:
                a_m = jnp.where((row_c ^ col_c) < 2 * m, a_m, 0.0)
            p_off = a_m if p_off is None else p_off + a_m
        out = out + jnp.dot(p_off.astype(BF16), v16, preferred_element_type=F32)

        w_rows = []
        for blk in range(C // SUBLANES):
            rb = blk * SUBLANES
            qb = q[rb:rb + SUBLANES]
            bb = b[rb:rb + SUBLANES]
            for s in range(SUBLANES):
                b_s = b_scr[pl.ds(rb + s, 1), :]
                k_s = k_scr[pl.ds(rb + s, 1), :]
                rel = bb - b_s
                if s:
                    rel = jnp.where(row8 >= s, rel, -jnp.inf)
                w_rows.append((qb * jnp.exp(rel)) * k_s)
        w_all = jnp.concatenate(w_rows, axis=0).astype(BF16)
        rs = jnp.dot(w_all, ones, preferred_element_type=F32)
        diag = []
        for blk in range(C // SUBLANES):
            rb = blk * SUBLANES
            acc = None
            for s in range(SUBLANES):
                off = (blk * SUBLANES + s) * SUBLANES
                term = rs[off:off + SUBLANES] * v_scr[pl.ds(rb + s, 1), :]
                acc = term if acc is None else acc + term
            diag.append(acc)
        out = out + jnp.concatenate(diag, axis=0)

        y = out * lax.rsqrt(jnp.mean(out * out, axis=-1, keepdims=True) + NORM_EPS) * head_gain
        o_ref[pl.ds(r0, C), :] = (y * _sigmoid(zg_ref[pl.ds(r0, C), :])).astype(o_ref.dtype)
        return carry

    lax.fori_loop(0, n_chunks, chunk, 0)


def _hgrn_scan(z, lb_logits, head_gain, batch, seq, layer, ts=1024):
    T, D4 = z.shape
    D = D4 // 4
    H = D // HGRN_HEAD
    n_s = seq // ts

    def zspec(group):
        return pl.BlockSpec((ts, HGRN_HEAD), lambda bi, hi, si: (bi * n_s + si, group * H + hi))

    n_layers = lb_logits.shape[0]
    return pl.pallas_call(
        functools.partial(_hgrn_scan_kernel, layer=layer, n_chunks=ts // HGRN_CHUNK),
        grid=(batch, H, n_s),
        in_specs=[zspec(0), zspec(1), zspec(2), zspec(3),
                  pl.BlockSpec((n_layers, HGRN_HEAD), lambda bi, hi, si: (0, hi)),
                  pl.BlockSpec((1, HGRN_HEAD), lambda bi, hi, si: (0, hi))],
        out_specs=pl.BlockSpec((ts, HGRN_HEAD), lambda bi, hi, si: (bi * n_s + si, hi)),
        out_shape=jax.ShapeDtypeStruct((T, D), BF16),
        scratch_shapes=[pltpu.VMEM((HGRN_HEAD, HGRN_HEAD), F32),
                        pltpu.VMEM((HGRN_CHUNK, HGRN_HEAD), F32),
                        pltpu.VMEM((HGRN_CHUNK, HGRN_HEAD), F32),
                        pltpu.VMEM((HGRN_CHUNK, HGRN_HEAD), F32)],
        compiler_params=_params(("arbitrary", "arbitrary", "arbitrary"), 32),
        name="hgrn_scan",
    )(z, z, z, z, lb_logits, head_gain)


def _swa_kernel(sink_ref, q_ref, kc_ref, kp_ref, vc_ref, vp_ref, o_ref, *, n_blocks):
    W = WINDOW
    pair = pl.program_id(1)
    first_tile = pl.program_id(2) == 0
    lane_k = lax.broadcasted_iota(jnp.int32, (2 * W, LANES), 1)
    qi = lax.broadcasted_iota(jnp.int32, (W, 2 * W), 0)
    kj = lax.broadcasted_iota(jnp.int32, (W, 2 * W), 1)
    in_window = (kj > qi) & (kj <= qi + W)

    for n in range(n_blocks):
        rows = slice(n * W, (n + 1) * W)
        q2 = jnp.concatenate([q_ref[rows, j * LANES:(j + 1) * LANES] for j in range(GROUP)], axis=0)
        if n == 0:
            k_prev, v_prev = kp_ref[...], vp_ref[...]
        else:
            k_prev, v_prev = kc_ref[(n - 1) * W:n * W, :], vc_ref[(n - 1) * W:n * W, :]
        kk = jnp.concatenate([k_prev, kc_ref[rows, :]], axis=0)
        vv = jnp.concatenate([v_prev, vc_ref[rows, :]], axis=0)
        zero = jnp.zeros_like(kk)
        k_bd = jnp.concatenate([jnp.where(lane_k < HEAD_DIM, kk, zero),
                                jnp.where(lane_k >= HEAD_DIM, kk, zero)], axis=0)
        v_bd = jnp.concatenate([jnp.where(lane_k < HEAD_DIM, vv, zero),
                                jnp.where(lane_k >= HEAD_DIM, vv, zero)], axis=0)
        s = lax.dot_general(q2, k_bd, (((1,), (1,)), ((), ())), preferred_element_type=F32)
        s = s * (HEAD_DIM ** -0.5)
        if n == 0:
            mask = in_window & (kj >= jnp.where(first_tile, W, 0))
        else:
            mask = in_window
        p_rows = []
        for j in range(GROUP):
            p_cols = []
            for gl in range(2):
                sub = jnp.where(mask, s[j * W:(j + 1) * W, gl * 2 * W:(gl + 1) * 2 * W], NEG_INF)
                sink = sink_ref[(2 * pair + gl) * GROUP + j]
                mx = jnp.maximum(jnp.max(sub, axis=-1, keepdims=True), sink)
                p = jnp.exp(sub - mx)
                den = jnp.sum(p, axis=-1, keepdims=True) + jnp.exp(sink - mx)
                p_cols.append((p * (1.0 / den)).astype(BF16))
            p_rows.append(jnp.concatenate(p_cols, axis=1))
        probs = jnp.concatenate(p_rows, axis=0)
        out = jnp.dot(probs, v_bd, preferred_element_type=F32)
        for j in range(GROUP):
            o_ref[rows, j * LANES:(j + 1) * LANES] = out[j * W:(j + 1) * W].astype(o_ref.dtype)


def _swa(q, kv, sinks, batch, seq, tq=512):
    T, DQ = q.shape
    n_pairs = kv.shape[1] // (2 * LANES)
    qw = DQ // n_pairs
    n_t = seq // tq
    wpt = tq // WINDOW

    def cur(col0):
        return pl.BlockSpec((tq, LANES), lambda bi, pi, ti: (bi * n_t + ti, col0 + pi))

    def prev(col0):
        return pl.BlockSpec(
            (WINDOW, LANES),
            lambda bi, pi, ti: (jnp.maximum((bi * n_t + ti) * wpt - 1, 0), col0 + pi))

    return pl.pallas_call(
        functools.partial(_swa_kernel, n_blocks=wpt),
        grid=(batch, n_pairs, n_t),
        in_specs=[pl.BlockSpec(memory_space=pltpu.SMEM),
                  pl.BlockSpec((tq, qw), lambda bi, pi, ti: (bi * n_t + ti, pi)),
                  cur(0), prev(0), cur(n_pairs), prev(n_pairs)],
        out_specs=pl.BlockSpec((tq, qw), lambda bi, pi, ti: (bi * n_t + ti, pi)),
        out_shape=jax.ShapeDtypeStruct((T, DQ), BF16),
        compiler_params=_params(("arbitrary", "arbitrary", "arbitrary"), 40),
        name="swa",
    )(sinks, q, kv, kv, kv, kv)


def _rope_lane_tables(seq):
    half = ROT_DIM // 2
    inv_freq = jnp.power(jnp.float32(ROPE_THETA), -jnp.arange(0, ROT_DIM, 2, dtype=F32) / ROT_DIM)
    ang = jnp.arange(seq, dtype=F32)[:, None] * inv_freq[None, :]
    sin, cos = jnp.sin(ang), jnp.cos(ang)
    ones = jnp.ones((seq, HEAD_DIM - ROT_DIM), F32)
    zeros_rest = jnp.zeros((seq, HEAD_DIM - ROT_DIM), F32)
    zeros_half = jnp.zeros((seq, half), F32)
    cos_h = jnp.concatenate([cos, cos, ones], axis=1)
    lo_h = jnp.concatenate([-sin, zeros_half, zeros_rest], axis=1)
    hi_h = jnp.concatenate([zeros_half, sin, zeros_rest], axis=1)
    reps = LANES // HEAD_DIM
    return tuple(jnp.tile(t, (1, reps)) for t in (cos_h, lo_h, hi_h))


def _pair_major_heads(n_q_heads):
    order = []
    for p in range(n_q_heads // (2 * GROUP)):
        for j in range(GROUP):
            for gl in range(2):
                order.append((2 * p + gl) * GROUP + j)
    return np.asarray(order, dtype=np.int32)


def kernel(x, c, norm_gain, w_ada, b_ada, w_ffn_in, w_ffn_out, w_hgrn_in, hgrn_lb_logits, hgrn_head_gain,
           w_hgrn_out, kv_gain, w_ada_kv, b_ada_kv, w_kv, b_kv, w_q, b_q, attn_sinks, w_attn_out, final_gain):
    B, S, D = x.shape
    T = B * S
    depth = w_ada.shape[0]
    n_a = w_hgrn_in.shape[0]
    d_ff = w_ffn_out.shape[2]
    ffp = -(-d_ff // 512) * 512
    n_q_heads = w_q.shape[2] // HEAD_DIM

    c_rows = jnp.pad(c, ((0, SUBLANES - B), (0, 0)))
    mod = _ada_mod(c_rows, w_ada, b_ada)[:, :B].reshape(depth, B, N_SUBLAYERS, 3, 1, D)
    mod_kv = _ada_mod(c_rows, w_ada_kv[None], b_ada_kv[None])[0, :B].reshape(B, 2, 1, D)

    w_in_p = jnp.pad(w_ffn_in.reshape(depth, 2, D, 2, d_ff),
                     ((0, 0),) * 4 + ((0, ffp - d_ff),)).astype(BF16).reshape(depth, 2, D, 2 * ffp)
    w_out_p = jnp.pad(w_ffn_out, ((0, 0), (0, 0), (0, ffp - d_ff), (0, 0))).astype(BF16)
    heads = _pair_major_heads(n_q_heads)
    rope = _rope_lane_tables(S)

    h = x.reshape(T, D)
    kv = None
    for layer in range(depth):
        def vecs(sub):
            return (norm_gain[layer, sub][None], mod[layer, :, sub, 0], mod[layer, :, sub, 1], mod[layer, :, sub, 2])

        gain, shift, scale, gate = vecs(0)
        h = _ffn(h, gain, shift, scale, gate, w_in_p[layer, 0], w_out_p[layer, 0], S)

        gain, shift, scale, gate = vecs(1)
        if layer < n_a:
            z = _norm_proj(h, gain, shift, scale, w_hgrn_in[layer].astype(BF16), S, F32, tn=1024)
            o = _hgrn_scan(z, hgrn_lb_logits, hgrn_head_gain[layer][None], B, S, layer)
            h = _proj_res(o, w_hgrn_out[layer].astype(BF16), h, gate, S)
        else:
            bl = layer - n_a
            wq = w_q[bl].reshape(D, n_q_heads, HEAD_DIM)[:, heads].reshape(D, -1).astype(BF16)
            bq = b_q[bl].reshape(n_q_heads, HEAD_DIM)[heads].reshape(1, -1)
            wo = w_attn_out[bl].reshape(n_q_heads, HEAD_DIM, D)[heads].reshape(-1, D).astype(BF16)
            q = _norm_proj(h, gain, shift, scale, wq, S, BF16, bias=bq, rope=rope, rot_cols=wq.shape[1])
            o = _swa(q, kv, attn_sinks[bl], B, S)
            h = _proj_res(o, wo, h, gate, S)

        gain, shift, scale, gate = vecs(2)
        last = layer == depth - 1
        h = _ffn(h, gain, shift, scale, gate, w_in_p[layer, 1], w_out_p[layer, 1], S,
                 final_gain=final_gain[None] if last else None)

        if layer == n_a - 1:
            kvw = w_kv.shape[1]
            kv = _norm_proj(h, kv_gain[None], mod_kv[:, 0], mod_kv[:, 1], w_kv.astype(BF16), S, BF16,
                            bias=b_kv[None], rope=rope, rot_cols=kvw // 2)

    return h.reshape(B, S, D)
```

```python
import functools

import numpy as np
import jax
import jax.numpy as jnp
from jax import lax
from jax.experimental import pallas as pl
from jax.experimental.pallas import tpu as pltpu

F32 = jnp.float32
BF16 = jnp.bfloat16

NORM_EPS = 1e-6
NEG_INF = -1e30
ROPE_THETA = 500000.0

LANES = 128
SUBLANES = 8
MIB = 1 << 20

HGRN_HEAD = 128
HGRN_CHUNK = 128
HEAD_DIM = 64
ROT_DIM = HEAD_DIM // 4
GROUP = 8
WINDOW = 128
N_SUBLAYERS = 3


def _params(semantics, vmem_mib):
    return pltpu.CompilerParams(dimension_semantics=semantics, vmem_limit_bytes=vmem_mib * MIB)


def _sigmoid(x):
    return 1.0 / (1.0 + jnp.exp(-x))


def _ada_norm(h, gain, shift, scale):
    y = h * lax.rsqrt(jnp.mean(h * h, axis=-1, keepdims=True) + NORM_EPS)
    return (y * gain) * (1.0 + scale) + shift


def _mod_kernel(c_ref, w_ref, b_ref, o_ref):
    c = c_ref[...]
    cs = (c * _sigmoid(c)).astype(BF16)
    o_ref[...] = jnp.dot(cs, w_ref[...].astype(BF16), preferred_element_type=F32) + b_ref[...]


def _ada_mod(c_rows, w, b):
    L, D, N = w.shape
    tn = 1024
    return pl.pallas_call(
        _mod_kernel,
        grid=(L, N // tn),
        in_specs=[
            pl.BlockSpec((SUBLANES, D), lambda l, j: (0, 0)),
            pl.BlockSpec((None, D, tn), lambda l, j: (l, 0, j)),
            pl.BlockSpec((None, 1, tn), lambda l, j: (l, 0, j)),
        ],
        out_specs=pl.BlockSpec((None, SUBLANES, tn), lambda l, j: (l, 0, j)),
        out_shape=jax.ShapeDtypeStruct((L, SUBLANES, N), F32),
        compiler_params=_params(("arbitrary", "arbitrary"), 40),
        name="ada_mod",
    )(c_rows, w, b.reshape(L, 1, N))


def _ffn_kernel(*refs, n_f, final_norm):
    if final_norm:
        (h_ref, gain_ref, shift_ref, scale_ref, gate_ref, wa_ref, wb_ref, wo_ref, fg_ref,
         o_ref, u_scr, acc_scr) = refs
    else:
        (h_ref, gain_ref, shift_ref, scale_ref, gate_ref, wa_ref, wb_ref, wo_ref,
         o_ref, u_scr, acc_scr) = refs
    j = pl.program_id(1)

    @pl.when(j == 0)
    def _():
        u_scr[...] = _ada_norm(h_ref[...], gain_ref[...], shift_ref[...], scale_ref[...]).astype(BF16)
        acc_scr[...] = jnp.zeros_like(acc_scr)

    u = u_scr[...]
    a = jnp.dot(u, wa_ref[...], preferred_element_type=F32)
    b = jnp.dot(u, wb_ref[...], preferred_element_type=F32)
    g = ((a * _sigmoid(a)) * b).astype(BF16)
    acc_scr[...] += jnp.dot(g, wo_ref[...], preferred_element_type=F32)

    @pl.when(j == n_f - 1)
    def _():
        out = h_ref[...] + (0.5 * gate_ref[...]) * acc_scr[...]
        if final_norm:
            out = out * lax.rsqrt(jnp.mean(out * out, axis=-1, keepdims=True) + NORM_EPS) * fg_ref[...]
        o_ref[...] = out


def _ffn(h, gain, shift, scale, gate, w_a_p, w_b_p, w_out_p, seq, final_gain=None, tm=512, tf=512):
    T, D = h.shape
    Fp = w_out_p.shape[0]
    n_f = Fp // tf
    per_b = seq // tm
    final_norm = final_gain is not None
    vec = pl.BlockSpec((1, D), lambda i, j: (0, 0))
    bvec = pl.BlockSpec((None, 1, D), lambda i, j: (i // per_b, 0, 0))
    in_specs = [
        pl.BlockSpec((tm, D), lambda i, j: (i, 0)),
        vec, bvec, bvec, bvec,
        pl.BlockSpec((D, tf), lambda i, j: (0, j)),
        pl.BlockSpec((D, tf), lambda i, j: (0, j)),
        pl.BlockSpec((tf, D), lambda i, j: (j, 0)),
    ]
    args = [h, gain, shift, scale, gate, w_a_p, w_b_p, w_out_p]
    if final_norm:
        in_specs.append(vec)
        args.append(final_gain)
    return pl.pallas_call(
        functools.partial(_ffn_kernel, n_f=n_f, final_norm=final_norm),
        grid=(T // tm, n_f),
        in_specs=in_specs,
        out_specs=pl.BlockSpec((tm, D), lambda i, j: (i, 0)),
        out_shape=jax.ShapeDtypeStruct((T, D), F32),
        scratch_shapes=[pltpu.VMEM((tm, D), BF16), pltpu.VMEM((tm, D), F32)],
        compiler_params=_params(("arbitrary", "arbitrary"), 56),
        name="ffn",
    )(*args)


def _rotate_block(z, cos, sin_lo, sin_hi):
    half = ROT_DIM // 2
    up = pltpu.roll(z, LANES - half, axis=1)
    down = pltpu.roll(z, half, axis=1)
    return z * cos + up * sin_lo + down * sin_hi


def _norm_proj_kernel(*refs, has_bias, rot_cols):
    refs = list(refs)
    h_ref, gain_ref, shift_ref, scale_ref, w_ref = refs[:5]
    pos = 5
    bias_ref = None
    if has_bias:
        bias_ref = refs[pos]
        pos += 1
    if rot_cols:
        cos_ref, slo_ref, shi_ref = refs[pos:pos + 3]
        pos += 3
    o_ref, u_scr = refs[pos], refs[pos + 1]

    @pl.when(pl.program_id(1) == 0)
    def _():
        u_scr[...] = _ada_norm(h_ref[...], gain_ref[...], shift_ref[...], scale_ref[...]).astype(BF16)

    z = jnp.dot(u_scr[...], w_ref[...], preferred_element_type=F32)
    if has_bias:
        z = z + bias_ref[...]
    if rot_cols:
        cos, slo, shi = cos_ref[...], slo_ref[...], shi_ref[...]
        tn = z.shape[1]
        blocks = []
        for cb in range(tn // LANES):
            zb = z[:, cb * LANES:(cb + 1) * LANES]
            if cb * LANES < rot_cols:
                zb = _rotate_block(zb, cos, slo, shi)
            blocks.append(zb)
        z = jnp.concatenate(blocks, axis=1) if len(blocks) > 1 else blocks[0]
    o_ref[...] = z.astype(o_ref.dtype)


def _norm_proj(h, gain, shift, scale, w, seq, out_dtype, bias=None, rope=None, rot_cols=0, tm=512, tn=None):
    T, D = h.shape
    N = w.shape[1]
    tn = N if tn is None else tn
    per_b = seq // tm
    vec = pl.BlockSpec((1, D), lambda i, j: (0, 0))
    bvec = pl.BlockSpec((None, 1, D), lambda i, j: (i // per_b, 0, 0))
    in_specs = [pl.BlockSpec((tm, D), lambda i, j: (i, 0)), vec, bvec, bvec,
                pl.BlockSpec((D, tn), lambda i, j: (0, j))]
    args = [h, gain, shift, scale, w]
    if bias is not None:
        in_specs.append(pl.BlockSpec((1, tn), lambda i, j: (0, j)))
        args.append(bias)
    if rot_cols:
        assert tn == N and rot_cols % LANES == 0
        tab = pl.BlockSpec((tm, LANES), lambda i, j: (i % per_b, 0))
        in_specs += [tab, tab, tab]
        args += list(rope)
    return pl.pallas_call(
        functools.partial(_norm_proj_kernel, has_bias=bias is not None, rot_cols=rot_cols),
        grid=(T // tm, N // tn),
        in_specs=in_specs,
        out_specs=pl.BlockSpec((tm, tn), lambda i, j: (i, j)),
        out_shape=jax.ShapeDtypeStruct((T, N), out_dtype),
        scratch_shapes=[pltpu.VMEM((tm, D), BF16)],
        compiler_params=_params(("arbitrary", "arbitrary"), 48),
        name="norm_proj",
    )(*args)


def _proj_res_kernel(x_ref, w_ref, h_ref, gate_ref, o_ref):
    y = jnp.dot(x_ref[...], w_ref[...], preferred_element_type=F32)
    o_ref[...] = h_ref[...] + gate_ref[...] * y


def _proj_res(x, w, h, gate, seq, tm=512):
    T, K = x.shape
    N = w.shape[1]
    per_b = seq // tm
    return pl.pallas_call(
        _proj_res_kernel,
        grid=(T // tm,),
        in_specs=[
            pl.BlockSpec((tm, K), lambda i: (i, 0)),
            pl.BlockSpec((K, N), lambda i: (0, 0)),
            pl.BlockSpec((tm, N), lambda i: (i, 0)),
            pl.BlockSpec((None, 1, N), lambda i: (i // per_b, 0, 0)),
        ],
        out_specs=pl.BlockSpec((tm, N), lambda i: (i, 0)),
        out_shape=jax.ShapeDtypeStruct((T, N), F32),
        compiler_params=_params(("arbitrary",), 48),
        name="proj_res",
    )(x, w, h, gate)


def _split_bf16x3(x):
    hi = x.astype(BF16)
    r1 = x - hi.astype(F32)
    mid = r1.astype(BF16)
    lo = (r1 - mid.astype(F32)).astype(BF16)
    return hi, mid, lo


def _hgrn_scan_kernel(zq_ref, zf_ref, zi_ref, zg_ref, lbl_ref, hg_ref, o_ref,
                      st_ref, b_scr, c_scr, *, layer, n_chunks, n_heads):
    C = HGRN_CHUNK
    K = HGRN_HEAD

    @pl.when(pl.program_id(2) == 0)
    def _():
        st_ref[...] = jnp.zeros_like(st_ref)

    row_c = lax.broadcasted_iota(jnp.int32, (C, C), 0)
    col_c = lax.broadcasted_iota(jnp.int32, (C, C), 1)
    tril = (row_c >= col_c).astype(BF16)
    row = lax.broadcasted_iota(jnp.int32, (C, K), 0)
    row8 = lax.broadcasted_iota(jnp.int32, (SUBLANES, K), 0)
    lane8 = lax.broadcasted_iota(jnp.int32, (SUBLANES, C), 1)
    ones = jnp.ones((K, K), BF16)
    levels = []
    m = C // 2
    while m >= SUBLANES:
        levels.append(m)
        m //= 2
    xor_c = row_c ^ col_c
    level_mask = {m: (row_c > col_c) & (xor_c >= m) & (xor_c < 2 * m) for m in levels}

    def head_chunk(hh, r0):
        lanes = slice(hh * K, (hh + 1) * K)
        logits = lbl_ref[:, lanes]
        e = jnp.exp(logits - jnp.max(logits, axis=0, keepdims=True))
        lb = jnp.sum(e[:layer + 1], axis=0, keepdims=True) / jnp.sum(e, axis=0, keepdims=True)

        zq = zq_ref[pl.ds(r0, C), lanes]
        q = zq * _sigmoid(zq)
        forget = lb + (1.0 - lb) * _sigmoid(zf_ref[pl.ds(r0, C), lanes])
        v16 = zi_ref[pl.ds(r0, C), lanes].astype(BF16)
        parts = jnp.concatenate(_split_bf16x3(jnp.log2(forget)), axis=1)
        sums = jnp.dot(tril, parts, preferred_element_type=F32)
        b = (sums[:, :K] + sums[:, K:2 * K]) + sums[:, 2 * K:]
        c = b - jnp.log2(1.0 - forget)
        b_scr[hh] = b
        c_scr[hh] = c
        st = st_ref[hh]

        qd = (q * jnp.exp2(b)).astype(BF16)
        out = lax.dot_general(qd, st.astype(BF16), (((1,), (1,)), ((), ())), preferred_element_type=F32)
        b_last = b_scr[hh, pl.ds(C - 1, 1), :]
        kd = jnp.exp2(b_last - c).astype(BF16)
        st_ref[hh] = st * jnp.exp2(b_last) + lax.dot_general(
            v16, kd, (((0,), (0,)), ((), ())), preferred_element_type=F32)

        w_rows = []
        for blk in range(C // SUBLANES):
            rb = blk * SUBLANES
            qb = q[rb:rb + SUBLANES]
            bb = b[rb:rb + SUBLANES]
            for s in range(SUBLANES):
                rel = bb - c_scr[hh, pl.ds(rb + s, 1), :]
                if s:
                    rel = jnp.where(row8 >= s, rel, -jnp.inf)
                w_rows.append(qb * jnp.exp2(rel))
        w_all = jnp.concatenate(w_rows, axis=0).astype(BF16)
        rs = jnp.dot(w_all, ones, preferred_element_type=F32)
        p_rows = []
        for blk in range(C // SUBLANES):
            rb = blk * SUBLANES
            p_blk = jnp.zeros((SUBLANES, C), F32)
            for s in range(SUBLANES):
                off = (blk * SUBLANES + s) * SUBLANES
                p_blk = jnp.where(lane8 == rb + s, rs[off:off + SUBLANES], p_blk)
            p_rows.append(p_blk)
        p = jnp.concatenate(p_rows, axis=0)

        for m in levels:
            anchors = []
            for pb in range(C // (2 * m)):
                a_row = b_scr[hh, pl.ds(pb * 2 * m + m - 1, 1), :]
                anchors.append(jnp.broadcast_to(a_row, (2 * m, K)))
            ba = jnp.concatenate(anchors, axis=0) if len(anchors) > 1 else anchors[0]
            later = (row & m) != 0
            ex = jnp.exp2(jnp.where(later, b - ba, ba - c))
            a_m = lax.dot_general((q * ex).astype(BF16), ex.astype(BF16), (((1,), (1,)), ((), ())),
                                  preferred_element_type=F32)
            p = jnp.where(level_mask[m], a_m, p)
        out = out + jnp.dot(p.astype(BF16), v16, preferred_element_type=F32)

        y = out * lax.rsqrt(jnp.mean(out * out, axis=-1, keepdims=True) + NORM_EPS) * hg_ref[:, lanes]
        o_ref[pl.ds(r0, C), lanes] = (y * _sigmoid(zg_ref[pl.ds(r0, C), lanes])).astype(o_ref.dtype)

    def chunk(ci, carry):
        r0 = pl.multiple_of(ci * C, C)
        for hh in range(n_heads):
            head_chunk(hh, r0)
        return carry

    lax.fori_loop(0, n_chunks, chunk, 0, unroll=2)


def _hgrn_scan(z, lb_logits, head_gain, batch, seq, layer, ts=1024, hpb=2):
    assert HGRN_CHUNK == HGRN_HEAD
    T, D4 = z.shape
    D = D4 // 4
    n_s = seq // ts
    hw = hpb * HGRN_HEAD
    n_hb = D // hw

    def zspec(group):
        return pl.BlockSpec((ts, hw), lambda bi, hi, si: (bi * n_s + si, group * n_hb + hi))

    n_layers = lb_logits.shape[0]
    return pl.pallas_call(
        functools.partial(_hgrn_scan_kernel, layer=layer, n_chunks=ts // HGRN_CHUNK, n_heads=hpb),
        grid=(batch, n_hb, n_s),
        in_specs=[zspec(0), zspec(1), zspec(2), zspec(3),
                  pl.BlockSpec((n_layers, hw), lambda bi, hi, si: (0, hi)),
                  pl.BlockSpec((1, hw), lambda bi, hi, si: (0, hi))],
        out_specs=pl.BlockSpec((ts, hw), lambda bi, hi, si: (bi * n_s + si, hi)),
        out_shape=jax.ShapeDtypeStruct((T, D), BF16),
        scratch_shapes=[pltpu.VMEM((hpb, HGRN_HEAD, HGRN_HEAD), F32),
                        pltpu.VMEM((hpb, HGRN_CHUNK, HGRN_HEAD), F32),
                        pltpu.VMEM((hpb, HGRN_CHUNK, HGRN_HEAD), F32)],
        compiler_params=_params(("arbitrary", "arbitrary", "arbitrary"), 32),
        name="hgrn_scan",
    )(z, z, z, z, lb_logits, head_gain)


def _swa_kernel(sink_ref, q_ref, kc_ref, kp_ref, vc_ref, vp_ref, o_ref, *, n_blocks):
    W = WINDOW
    pair = pl.program_id(1)
    first_tile = pl.program_id(2) == 0
    lane_k = lax.broadcasted_iota(jnp.int32, (2 * W, LANES), 1)
    qi = lax.broadcasted_iota(jnp.int32, (W, 2 * W), 0)
    kj = lax.broadcasted_iota(jnp.int32, (W, 2 * W), 1)
    in_window = (kj > qi) & (kj <= qi + W)

    for n in range(n_blocks):
        rows = slice(n * W, (n + 1) * W)
        q2 = jnp.concatenate([q_ref[rows, j * LANES:(j + 1) * LANES] for j in range(GROUP)], axis=0)
        if n == 0:
            k_prev, v_prev = kp_ref[...], vp_ref[...]
        else:
            k_prev, v_prev = kc_ref[(n - 1) * W:n * W, :], vc_ref[(n - 1) * W:n * W, :]
        kk = jnp.concatenate([k_prev, kc_ref[rows, :]], axis=0)
        vv = jnp.concatenate([v_prev, vc_ref[rows, :]], axis=0)
        zero = jnp.zeros_like(kk)
        k_bd = jnp.concatenate([jnp.where(lane_k < HEAD_DIM, kk, zero),
                                jnp.where(lane_k >= HEAD_DIM, kk, zero)], axis=0)
        v_bd = jnp.concatenate([jnp.where(lane_k < HEAD_DIM, vv, zero),
                                jnp.where(lane_k >= HEAD_DIM, vv, zero)], axis=0)
        s = lax.dot_general(q2, k_bd, (((1,), (1,)), ((), ())), preferred_element_type=F32)
        s = s * (HEAD_DIM ** -0.5)
        if n == 0:
            mask = in_window & (kj >= jnp.where(first_tile, W, 0))
        else:
            mask = in_window
        p_rows = []
        for j in range(GROUP):
            p_cols = []
            for gl in range(2):
                sub = jnp.where(mask, s[j * W:(j + 1) * W, gl * 2 * W:(gl + 1) * 2 * W], NEG_INF)
                sink = sink_ref[(2 * pair + gl) * GROUP + j]
                mx = jnp.maximum(jnp.max(sub, axis=-1, keepdims=True), sink)
                p = jnp.exp(sub - mx)
                den = jnp.sum(p, axis=-1, keepdims=True) + jnp.exp(sink - mx)
                p_cols.append((p * (1.0 / den)).astype(BF16))
            p_rows.append(jnp.concatenate(p_cols, axis=1))
        probs = jnp.concatenate(p_rows, axis=0)
        out = jnp.dot(probs, v_bd, preferred_element_type=F32)
        for j in range(GROUP):
            o_ref[rows, j * LANES:(j + 1) * LANES] = out[j * W:(j + 1) * W].astype(o_ref.dtype)


def _swa(q, kv, sinks, batch, seq, tq=512):
    T, DQ = q.shape
    n_pairs = kv.shape[1] // (2 * LANES)
    qw = DQ // n_pairs
    n_t = seq // tq
    wpt = tq // WINDOW

    def cur(col0):
        return pl.BlockSpec((tq, LANES), lambda bi, pi, ti: (bi * n_t + ti, col0 + pi))

    def prev(col0):
        return pl.BlockSpec(
            (WINDOW, LANES),
            lambda bi, pi, ti: (jnp.maximum((bi * n_t + ti) * wpt - 1, 0), col0 + pi))

    return pl.pallas_call(
        functools.partial(_swa_kernel, n_blocks=wpt),
        grid=(batch, n_pairs, n_t),
        in_specs=[pl.BlockSpec(memory_space=pltpu.SMEM),
                  pl.BlockSpec((tq, qw), lambda bi, pi, ti: (bi * n_t + ti, pi)),
                  cur(0), prev(0), cur(n_pairs), prev(n_pairs)],
        out_specs=pl.BlockSpec((tq, qw), lambda bi, pi, ti: (bi * n_t + ti, pi)),
        out_shape=jax.ShapeDtypeStruct((T, DQ), BF16),
        compiler_params=_params(("arbitrary", "arbitrary", "arbitrary"), 40),
        name="swa",
    )(sinks, q, kv, kv, kv, kv)


def _rope_lane_tables(seq):
    half = ROT_DIM // 2
    inv_freq = jnp.power(jnp.float32(ROPE_THETA), -jnp.arange(0, ROT_DIM, 2, dtype=F32) / ROT_DIM)
    ang = jnp.arange(seq, dtype=F32)[:, None] * inv_freq[None, :]
    sin, cos = jnp.sin(ang), jnp.cos(ang)
    ones = jnp.ones((seq, HEAD_DIM - ROT_DIM), F32)
    zeros_rest = jnp.zeros((seq, HEAD_DIM - ROT_DIM), F32)
    zeros_half = jnp.zeros((seq, half), F32)
    cos_h = jnp.concatenate([cos, cos, ones], axis=1)
    lo_h = jnp.concatenate([-sin, zeros_half, zeros_rest], axis=1)
    hi_h = jnp.concatenate([zeros_half, sin, zeros_rest], axis=1)
    reps = LANES // HEAD_DIM
    return tuple(jnp.tile(t, (1, reps)) for t in (cos_h, lo_h, hi_h))


def _pair_major_heads(n_q_heads):
    order = []
    for p in range(n_q_heads // (2 * GROUP)):
        for j in range(GROUP):
            for gl in range(2):
                order.append((2 * p + gl) * GROUP + j)
    return np.asarray(order, dtype=np.int32)


def kernel(x, c, norm_gain, w_ada, b_ada, w_ffn_in, w_ffn_out, w_hgrn_in, hgrn_lb_logits, hgrn_head_gain,
           w_hgrn_out, kv_gain, w_ada_kv, b_ada_kv, w_kv, b_kv, w_q, b_q, attn_sinks, w_attn_out, final_gain):
    B, S, D = x.shape
    T = B * S
    depth = w_ada.shape[0]
    n_a = w_hgrn_in.shape[0]
    d_ff = w_ffn_out.shape[2]
    ffp = -(-d_ff // 512) * 512
    n_q_heads = w_q.shape[2] // HEAD_DIM

    c_rows = jnp.pad(c, ((0, SUBLANES - B), (0, 0)))
    mod = _ada_mod(c_rows, w_ada, b_ada)[:, :B].reshape(depth, B, N_SUBLAYERS, 3, 1, D)
    mod_kv = _ada_mod(c_rows, w_ada_kv[None], b_ada_kv[None])[0, :B].reshape(B, 2, 1, D)

    col_pad = jnp.zeros((depth, 2, D, ffp - d_ff), BF16)
    w_a_p = jnp.concatenate([w_ffn_in[..., :d_ff].astype(BF16), col_pad], axis=-1)
    w_b_p = jnp.concatenate([w_ffn_in[..., d_ff:].astype(BF16), col_pad], axis=-1)
    w_out_p = jnp.concatenate([w_ffn_out.astype(BF16), jnp.zeros((depth, 2, ffp - d_ff, D), BF16)], axis=2)
    heads = _pair_major_heads(n_q_heads)
    rope = _rope_lane_tables(S)

    h = x.reshape(T, D)
    kv = None
    for layer in range(depth):
        def vecs(sub):
            return (norm_gain[layer, sub][None], mod[layer, :, sub, 0], mod[layer, :, sub, 1], mod[layer, :, sub, 2])

        gain, shift, scale, gate = vecs(0)
        h = _ffn(h, gain, shift, scale, gate, w_a_p[layer, 0], w_b_p[layer, 0], w_out_p[layer, 0], S)

        gain, shift, scale, gate = vecs(1)
        if layer < n_a:
            z = _norm_proj(h, gain, shift, scale, w_hgrn_in[layer].astype(BF16), S, F32, tn=1024)
            o = _hgrn_scan(z, hgrn_lb_logits, hgrn_head_gain[layer][None], B, S, layer)
            h = _proj_res(o, w_hgrn_out[layer].astype(BF16), h, gate, S)
        else:
            bl = layer - n_a
            wq = w_q[bl].reshape(D, n_q_heads, HEAD_DIM)[:, heads].reshape(D, -1).astype(BF16)
            bq = b_q[bl].reshape(n_q_heads, HEAD_DIM)[heads].reshape(1, -1)
            wo = w_attn_out[bl].reshape(n_q_heads, HEAD_DIM, D)[heads].reshape(-1, D).astype(BF16)
            q = _norm_proj(h, gain, shift, scale, wq, S, BF16, bias=bq, rope=rope, rot_cols=wq.shape[1])
            o = _swa(q, kv, attn_sinks[bl], B, S)
            h = _proj_res(o, wo, h, gate, S)

        gain, shift, scale, gate = vecs(2)
        last = layer == depth - 1
        h = _ffn(h, gain, shift, scale, gate, w_a_p[layer, 1], w_b_p[layer, 1], w_out_p[layer, 1], S,
                 final_gain=final_gain[None] if last else None)

        if layer == n_a - 1:
            kvw = w_kv.shape[1]
            kv = _norm_proj(h, kv_gain[None], mod_kv[:, 0], mod_kv[:, 1], w_kv.astype(BF16), S, BF16,
                            bias=b_kv[None], rope=rope, rot_cols=kvw // 2)

    return h.reshape(B, S, D)
```

```python
import functools

import numpy as np
import jax
import jax.numpy as jnp
from jax import lax
from jax.experimental import pallas as pl
from jax.experimental.pallas import tpu as pltpu

F32 = jnp.float32
BF16 = jnp.bfloat16

NORM_EPS = 1e-6
NEG_INF = -1e30
ROPE_THETA = 500000.0

LANES = 128
SUBLANES = 8
MIB = 1 << 20

HGRN_HEAD = 128
HGRN_CHUNK = 128
HEAD_DIM = 64
ROT_DIM = HEAD_DIM // 4
GROUP = 8
WINDOW = 128
N_SUBLAYERS = 3


def _params(semantics, vmem_mib):
    return pltpu.CompilerParams(dimension_semantics=semantics, vmem_limit_bytes=vmem_mib * MIB)


def _sigmoid(x):
    return 1.0 / (1.0 + jnp.exp(-x))


def _ada_norm(h, gain, shift, scale):
    y = h * lax.rsqrt(jnp.mean(h * h, axis=-1, keepdims=True) + NORM_EPS)
    return (y * gain) * (1.0 + scale) + shift


NORM_ROWS = 32
PACK_ROWS = 16


def _ada_norm_into(u_ref, h_ref, gain_ref, shift_ref, scale_ref, inv_scr, zero_ref=None):
    n_rows, d = h_ref.shape
    n_cb = d // LANES

    def pass1(i, carry):
        r = pl.multiple_of(i * SUBLANES, SUBLANES)
        acc = None
        for cb in range(n_cb):
            xb = h_ref[pl.ds(r, SUBLANES), cb * LANES:(cb + 1) * LANES]
            acc = xb * xb if acc is None else acc + xb * xb
        inv_scr[pl.ds(r, SUBLANES), :] = acc
        return carry
    lax.fori_loop(0, n_rows // SUBLANES, pass1, 0, unroll=4)
    ones = jnp.ones((LANES, LANES), BF16)
    ssq = None
    for part in _split_bf16x3(inv_scr[...]):
        s = jnp.dot(part, ones, preferred_element_type=F32)
        ssq = s if ssq is None else ssq + s
    inv_scr[...] = lax.rsqrt(ssq * (1.0 / d) + NORM_EPS)

    mult = gain_ref[...] * (1.0 + scale_ref[...])
    shift = shift_ref[...]

    def pass2(i, carry):
        r = pl.multiple_of(i * PACK_ROWS, PACK_ROWS)
        rows = pl.ds(r, PACK_ROWS)
        inv = inv_scr[rows, :]
        for cb in range(n_cb):
            cols = slice(cb * LANES, (cb + 1) * LANES)
            u_ref[rows, cols] = ((h_ref[rows, cols] * inv) * mult[:, cols] + shift[:, cols]).astype(u_ref.dtype)
            if zero_ref is not None:
                zero_ref[rows, cols] = jnp.zeros((PACK_ROWS, LANES), zero_ref.dtype)
        return carry
    lax.fori_loop(0, n_rows // PACK_ROWS, pass2, 0, unroll=2)


def _mod_kernel(c_ref, w_ref, b_ref, o_ref):
    c = c_ref[...]
    cs = (c * _sigmoid(c)).astype(BF16)
    o_ref[...] = jnp.dot(cs, w_ref[...].astype(BF16), preferred_element_type=F32) + b_ref[...]


def _ada_mod(c_rows, w, b):
    L, D, N = w.shape
    tn = 1024
    return pl.pallas_call(
        _mod_kernel,
        grid=(L, N // tn),
        in_specs=[
            pl.BlockSpec((SUBLANES, D), lambda l, j: (0, 0)),
            pl.BlockSpec((None, D, tn), lambda l, j: (l, 0, j)),
            pl.BlockSpec((None, 1, tn), lambda l, j: (l, 0, j)),
        ],
        out_specs=pl.BlockSpec((None, SUBLANES, tn), lambda l, j: (l, 0, j)),
        out_shape=jax.ShapeDtypeStruct((L, SUBLANES, N), F32),
        compiler_params=_params(("arbitrary", "arbitrary"), 40),
        name="ada_mod",
    )(c_rows, w, b.reshape(L, 1, N))


def _ffn_kernel(*refs, n_f, final_norm):
    if final_norm:
        (h_ref, gain_ref, shift_ref, scale_ref, gate_ref, wa_ref, wb_ref, wo_ref, fg_ref,
         o_ref, u_scr, inv_scr) = refs
    else:
        (h_ref, gain_ref, shift_ref, scale_ref, gate_ref, wa_ref, wb_ref, wo_ref,
         o_ref, u_scr, inv_scr) = refs
    j = pl.program_id(1)

    @pl.when(j == 0)
    def _():
        _ada_norm_into(u_scr, h_ref, gain_ref, shift_ref, scale_ref, inv_scr, zero_ref=o_ref)

    u = u_scr[...]
    a = jnp.dot(u, wa_ref[...], preferred_element_type=F32)
    b = jnp.dot(u, wb_ref[...], preferred_element_type=F32)
    g = ((a * _sigmoid(a)) * b).astype(BF16)
    o_ref[...] += jnp.dot(g, wo_ref[...], preferred_element_type=F32)

    @pl.when(j == n_f - 1)
    def _():
        def body(i, carry):
            r = pl.multiple_of(i * NORM_ROWS, NORM_ROWS)
            rows = pl.ds(r, NORM_ROWS)
            out = h_ref[rows, :] + (0.5 * gate_ref[...]) * o_ref[rows, :]
            if final_norm:
                out = out * lax.rsqrt(jnp.mean(out * out, axis=-1, keepdims=True) + NORM_EPS) * fg_ref[...]
            o_ref[rows, :] = out
            return carry
        lax.fori_loop(0, o_ref.shape[0] // NORM_ROWS, body, 0, unroll=2)


def _ffn(h, gain, shift, scale, gate, w_a_p, w_b_p, w_out_p, seq, final_gain=None, tm=1024, tf=512):
    T, D = h.shape
    Fp = w_out_p.shape[0]
    n_f = Fp // tf
    per_b = seq // tm
    final_norm = final_gain is not None
    vec = pl.BlockSpec((1, D), lambda i, j: (0, 0))
    bvec = pl.BlockSpec((None, 1, D), lambda i, j: (i // per_b, 0, 0))
    in_specs = [
        pl.BlockSpec((tm, D), lambda i, j: (i, 0)),
        vec, bvec, bvec, bvec,
        pl.BlockSpec((D, tf), lambda i, j: (0, j)),
        pl.BlockSpec((D, tf), lambda i, j: (0, j)),
        pl.BlockSpec((tf, D), lambda i, j: (j, 0)),
    ]
    args = [h, gain, shift, scale, gate, w_a_p, w_b_p, w_out_p]
    if final_norm:
        in_specs.append(vec)
        args.append(final_gain)
    return pl.pallas_call(
        functools.partial(_ffn_kernel, n_f=n_f, final_norm=final_norm),
        grid=(T // tm, n_f),
        in_specs=in_specs,
        out_specs=pl.BlockSpec((tm, D), lambda i, j: (i, 0), pipeline_mode=pl.Buffered(1)),
        out_shape=jax.ShapeDtypeStruct((T, D), F32),
        scratch_shapes=[pltpu.VMEM((tm, D), BF16), pltpu.VMEM((tm, LANES), F32)],
        compiler_params=_params(("arbitrary", "arbitrary"), 56),
        name="ffn",
    )(*args)


def _rotate_block(z, cos, sin_lo, sin_hi):
    half = ROT_DIM // 2
    up = pltpu.roll(z, LANES - half, axis=1)
    down = pltpu.roll(z, half, axis=1)
    return z * cos + up * sin_lo + down * sin_hi


def _norm_proj_kernel(*refs, has_bias, rot_cols):
    refs = list(refs)
    h_ref, gain_ref, shift_ref, scale_ref, w_ref = refs[:5]
    pos = 5
    bias_ref = None
    if has_bias:
        bias_ref = refs[pos]
        pos += 1
    if rot_cols:
        cos_ref, slo_ref, shi_ref = refs[pos:pos + 3]
        pos += 3
    o_ref, u_scr, inv_scr = refs[pos:pos + 3]

    @pl.when(pl.program_id(1) == 0)
    def _():
        _ada_norm_into(u_scr, h_ref, gain_ref, shift_ref, scale_ref, inv_scr)

    z = jnp.dot(u_scr[...], w_ref[...], preferred_element_type=F32)
    if has_bias:
        z = z + bias_ref[...]
    if rot_cols:
        cos, slo, shi = cos_ref[...], slo_ref[...], shi_ref[...]
        tn = z.shape[1]
        blocks = []
        for cb in range(tn // LANES):
            zb = z[:, cb * LANES:(cb + 1) * LANES]
            if cb * LANES < rot_cols:
                zb = _rotate_block(zb, cos, slo, shi)
            blocks.append(zb)
        z = jnp.concatenate(blocks, axis=1) if len(blocks) > 1 else blocks[0]
    o_ref[...] = z.astype(o_ref.dtype)


def _norm_proj(h, gain, shift, scale, w, seq, out_dtype, bias=None, rope=None, rot_cols=0, tm=1024, tn=None):
    T, D = h.shape
    N = w.shape[1]
    tn = N if tn is None else tn
    per_b = seq // tm
    vec = pl.BlockSpec((1, D), lambda i, j: (0, 0))
    bvec = pl.BlockSpec((None, 1, D), lambda i, j: (i // per_b, 0, 0))
    in_specs = [pl.BlockSpec((tm, D), lambda i, j: (i, 0)), vec, bvec, bvec,
                pl.BlockSpec((D, tn), lambda i, j: (0, j))]
    args = [h, gain, shift, scale, w]
    if bias is not None:
        in_specs.append(pl.BlockSpec((1, tn), lambda i, j: (0, j)))
        args.append(bias)
    if rot_cols:
        assert tn == N and rot_cols % LANES == 0
        tab = pl.BlockSpec((tm, LANES), lambda i, j: (i % per_b, 0))
        in_specs += [tab, tab, tab]
        args += list(rope)
    return pl.pallas_call(
        functools.partial(_norm_proj_kernel, has_bias=bias is not None, rot_cols=rot_cols),
        grid=(T // tm, N // tn),
        in_specs=in_specs,
        out_specs=pl.BlockSpec((tm, tn), lambda i, j: (i, j)),
        out_shape=jax.ShapeDtypeStruct((T, N), out_dtype),
        scratch_shapes=[pltpu.VMEM((tm, D), BF16), pltpu.VMEM((tm, LANES), F32)],
        compiler_params=_params(("arbitrary", "arbitrary"), 48),
        name="norm_proj",
    )(*args)


def _proj_res_kernel(x_ref, w_ref, h_ref, gate_ref, o_ref):
    y = jnp.dot(x_ref[...], w_ref[...], preferred_element_type=F32)
    o_ref[...] = h_ref[...] + gate_ref[...] * y


def _proj_res(x, w, h, gate, seq, tm=512):
    T, K = x.shape
    N = w.shape[1]
    per_b = seq // tm
    return pl.pallas_call(
        _proj_res_kernel,
        grid=(T // tm,),
        in_specs=[
            pl.BlockSpec((tm, K), lambda i: (i, 0)),
            pl.BlockSpec((K, N), lambda i: (0, 0)),
            pl.BlockSpec((tm, N), lambda i: (i, 0)),
            pl.BlockSpec((None, 1, N), lambda i: (i // per_b, 0, 0)),
        ],
        out_specs=pl.BlockSpec((tm, N), lambda i: (i, 0)),
        out_shape=jax.ShapeDtypeStruct((T, N), F32),
        compiler_params=_params(("arbitrary",), 48),
        name="proj_res",
    )(x, w, h, gate)


def _split_bf16x3(x):
    hi = x.astype(BF16)
    r1 = x - hi.astype(F32)
    mid = r1.astype(BF16)
    lo = (r1 - mid.astype(F32)).astype(BF16)
    return hi, mid, lo


def _hgrn_scan_kernel(zq_ref, zf_ref, zi_ref, zg_ref, lbl_ref, hg_ref, o_ref,
                      st_ref, b_scr, c_scr, *, layer, n_chunks, n_heads):
    C = HGRN_CHUNK
    K = HGRN_HEAD

    @pl.when(pl.program_id(2) == 0)
    def _():
        st_ref[...] = jnp.zeros_like(st_ref)

    row_c = lax.broadcasted_iota(jnp.int32, (C, C), 0)
    col_c = lax.broadcasted_iota(jnp.int32, (C, C), 1)
    tril = (row_c >= col_c).astype(BF16)
    row = lax.broadcasted_iota(jnp.int32, (C, K), 0)
    row8 = lax.broadcasted_iota(jnp.int32, (SUBLANES, K), 0)
    lane8 = lax.broadcasted_iota(jnp.int32, (SUBLANES, C), 1)
    ones = jnp.ones((K, K), BF16)
    levels = []
    m = C // 2
    while m >= SUBLANES:
        levels.append(m)
        m //= 2
    xor_c = row_c ^ col_c
    level_mask = {m: (row_c > col_c) & (xor_c >= m) & (xor_c < 2 * m) for m in levels}

    def head_chunk(hh, r0):
        lanes = slice(hh * K, (hh + 1) * K)
        logits = lbl_ref[:, lanes]
        e = jnp.exp(logits - jnp.max(logits, axis=0, keepdims=True))
        lb = jnp.sum(e[:layer + 1], axis=0, keepdims=True) / jnp.sum(e, axis=0, keepdims=True)

        zq = zq_ref[pl.ds(r0, C), lanes]
        q = zq * _sigmoid(zq)
        forget = lb + (1.0 - lb) * _sigmoid(zf_ref[pl.ds(r0, C), lanes])
        v16 = zi_ref[pl.ds(r0, C), lanes].astype(BF16)
        parts = jnp.concatenate(_split_bf16x3(jnp.log2(forget)), axis=1)
        sums = jnp.dot(tril, parts, preferred_element_type=F32)
        b = (sums[:, :K] + sums[:, K:2 * K]) + sums[:, 2 * K:]
        c = b - jnp.log2(1.0 - forget)
        b_scr[hh] = b
        c_scr[hh] = c
        st = st_ref[hh]

        qd = (q * jnp.exp2(b)).astype(BF16)
        out = lax.dot_general(qd, st.astype(BF16), (((1,), (1,)), ((), ())), preferred_element_type=F32)
        b_last = b_scr[hh, pl.ds(C - 1, 1), :]
        kd = jnp.exp2(b_last - c).astype(BF16)
        st_ref[hh] = st * jnp.exp2(b_last) + lax.dot_general(
            v16, kd, (((0,), (0,)), ((), ())), preferred_element_type=F32)

        w_rows = []
        for blk in range(C // SUBLANES):
            rb = blk * SUBLANES
            qb = q[rb:rb + SUBLANES]
            bb = b[rb:rb + SUBLANES]
            for s in range(SUBLANES):
                rel = bb - c_scr[hh, pl.ds(rb + s, 1), :]
                if s:
                    rel = jnp.where(row8 >= s, rel, -jnp.inf)
                w_rows.append(qb * jnp.exp2(rel))
        w_all = jnp.concatenate(w_rows, axis=0).astype(BF16)
        rs = jnp.dot(w_all, ones, preferred_element_type=F32)
        p_rows = []
        for blk in range(C // SUBLANES):
            rb = blk * SUBLANES
            p_blk = jnp.zeros((SUBLANES, C), F32)
            for s in range(SUBLANES):
                off = (blk * SUBLANES + s) * SUBLANES
                p_blk = jnp.where(lane8 == rb + s, rs[off:off + SUBLANES], p_blk)
            p_rows.append(p_blk)
        p = jnp.concatenate(p_rows, axis=0)

        for m in levels:
            anchors = []
            for pb in range(C // (2 * m)):
                a_row = b_scr[hh, pl.ds(pb * 2 * m + m - 1, 1), :]
                anchors.append(jnp.broadcast_to(a_row, (2 * m, K)))
            ba = jnp.concatenate(anchors, axis=0) if len(anchors) > 1 else anchors[0]
            later = (row & m) != 0
            ex = jnp.exp2(jnp.where(later, b - ba, ba - c))
            a_m = lax.dot_general((q * ex).astype(BF16), ex.astype(BF16), (((1,), (1,)), ((), ())),
                                  preferred_element_type=F32)
            p = jnp.where(level_mask[m], a_m, p)
        out = out + jnp.dot(p.astype(BF16), v16, preferred_element_type=F32)

        y = out * lax.rsqrt(jnp.mean(out * out, axis=-1, keepdims=True) + NORM_EPS) * hg_ref[:, lanes]
        o_ref[pl.ds(r0, C), lanes] = (y * _sigmoid(zg_ref[pl.ds(r0, C), lanes])).astype(o_ref.dtype)

    def chunk(ci, carry):
        r0 = pl.multiple_of(ci * C, C)
        for hh in range(n_heads):
            head_chunk(hh, r0)
        return carry

    lax.fori_loop(0, n_chunks, chunk, 0, unroll=2)


def _hgrn_scan(z, lb_logits, head_gain, batch, seq, layer, ts=1024, hpb=2):
    assert HGRN_CHUNK == HGRN_HEAD
    T, D4 = z.shape
    D = D4 // 4
    n_s = seq // ts
    hw = hpb * HGRN_HEAD
    n_hb = D // hw

    def zspec(group):
        return pl.BlockSpec((ts, hw), lambda bi, hi, si: (bi * n_s + si, group * n_hb + hi))

    n_layers = lb_logits.shape[0]
    return pl.pallas_call(
        functools.partial(_hgrn_scan_kernel, layer=layer, n_chunks=ts // HGRN_CHUNK, n_heads=hpb),
        grid=(batch, n_hb, n_s),
        in_specs=[zspec(0), zspec(1), zspec(2), zspec(3),
                  pl.BlockSpec((n_layers, hw), lambda bi, hi, si: (0, hi)),
                  pl.BlockSpec((1, hw), lambda bi, hi, si: (0, hi))],
        out_specs=pl.BlockSpec((ts, hw), lambda bi, hi, si: (bi * n_s + si, hi)),
        out_shape=jax.ShapeDtypeStruct((T, D), BF16),
        scratch_shapes=[pltpu.VMEM((hpb, HGRN_HEAD, HGRN_HEAD), F32),
                        pltpu.VMEM((hpb, HGRN_CHUNK, HGRN_HEAD), F32),
                        pltpu.VMEM((hpb, HGRN_CHUNK, HGRN_HEAD), F32)],
        compiler_params=_params(("arbitrary", "arbitrary", "arbitrary"), 32),
        name="hgrn_scan",
    )(z, z, z, z, lb_logits, head_gain)


def _swa_kernel(sink_ref, q_ref, kc_ref, kp_ref, vc_ref, vp_ref, o_ref, *, n_blocks):
    W = WINDOW
    pair = pl.program_id(1)
    first_tile = pl.program_id(2) == 0
    lane_k = lax.broadcasted_iota(jnp.int32, (2 * W, LANES), 1)
    qi = lax.broadcasted_iota(jnp.int32, (W, 2 * W), 0)
    kj = lax.broadcasted_iota(jnp.int32, (W, 2 * W), 1)
    in_window = (kj > qi) & (kj <= qi + W)

    for n in range(n_blocks):
        rows = slice(n * W, (n + 1) * W)
        q2 = jnp.concatenate([q_ref[rows, j * LANES:(j + 1) * LANES] for j in range(GROUP)], axis=0)
        if n == 0:
            k_prev, v_prev = kp_ref[...], vp_ref[...]
        else:
            k_prev, v_prev = kc_ref[(n - 1) * W:n * W, :], vc_ref[(n - 1) * W:n * W, :]
        kk = jnp.concatenate([k_prev, kc_ref[rows, :]], axis=0)
        vv = jnp.concatenate([v_prev, vc_ref[rows, :]], axis=0)
        zero = jnp.zeros_like(kk)
        k_bd = jnp.concatenate([jnp.where(lane_k < HEAD_DIM, kk, zero),
                                jnp.where(lane_k >= HEAD_DIM, kk, zero)], axis=0)
        v_bd = jnp.concatenate([jnp.where(lane_k < HEAD_DIM, vv, zero),
                                jnp.where(lane_k >= HEAD_DIM, vv, zero)], axis=0)
        s = lax.dot_general(q2, k_bd, (((1,), (1,)), ((), ())), preferred_element_type=F32)
        s = s * (HEAD_DIM ** -0.5)
        if n == 0:
            mask = in_window & (kj >= jnp.where(first_tile, W, 0))
        else:
            mask = in_window
        p_rows = []
        for j in range(GROUP):
            p_cols = []
            for gl in range(2):
                sub = jnp.where(mask, s[j * W:(j + 1) * W, gl * 2 * W:(gl + 1) * 2 * W], NEG_INF)
                sink = sink_ref[(2 * pair + gl) * GROUP + j]
                mx = jnp.maximum(jnp.max(sub, axis=-1, keepdims=True), sink)
                p = jnp.exp(sub - mx)
                den = jnp.sum(p, axis=-1, keepdims=True) + jnp.exp(sink - mx)
                p_cols.append((p * (1.0 / den)).astype(BF16))
            p_rows.append(jnp.concatenate(p_cols, axis=1))
        probs = jnp.concatenate(p_rows, axis=0)
        out = jnp.dot(probs, v_bd, preferred_element_type=F32)
        for j in range(GROUP):
            o_ref[rows, j * LANES:(j + 1) * LANES] = out[j * W:(j + 1) * W].astype(o_ref.dtype)


def _swa(q, kv, sinks, batch, seq, tq=512):
    T, DQ = q.shape
    n_pairs = kv.shape[1] // (2 * LANES)
    qw = DQ // n_pairs
    n_t = seq // tq
    wpt = tq // WINDOW

    def cur(col0):
        return pl.BlockSpec((tq, LANES), lambda bi, pi, ti: (bi * n_t + ti, col0 + pi))

    def prev(col0):
        return pl.BlockSpec(
            (WINDOW, LANES),
            lambda bi, pi, ti: (jnp.maximum((bi * n_t + ti) * wpt - 1, 0), col0 + pi))

    return pl.pallas_call(
        functools.partial(_swa_kernel, n_blocks=wpt),
        grid=(batch, n_pairs, n_t),
        in_specs=[pl.BlockSpec(memory_space=pltpu.SMEM),
                  pl.BlockSpec((tq, qw), lambda bi, pi, ti: (bi * n_t + ti, pi)),
                  cur(0), prev(0), cur(n_pairs), prev(n_pairs)],
        out_specs=pl.BlockSpec((tq, qw), lambda bi, pi, ti: (bi * n_t + ti, pi)),
        out_shape=jax.ShapeDtypeStruct((T, DQ), BF16),
        compiler_params=_params(("arbitrary", "arbitrary", "arbitrary"), 40),
        name="swa",
    )(sinks, q, kv, kv, kv, kv)


def _rope_lane_tables(seq):
    half = ROT_DIM // 2
    inv_freq = jnp.power(jnp.float32(ROPE_THETA), -jnp.arange(0, ROT_DIM, 2, dtype=F32) / ROT_DIM)
    ang = jnp.arange(seq, dtype=F32)[:, None] * inv_freq[None, :]
    sin, cos = jnp.sin(ang), jnp.cos(ang)
    ones = jnp.ones((seq, HEAD_DIM - ROT_DIM), F32)
    zeros_rest = jnp.zeros((seq, HEAD_DIM - ROT_DIM), F32)
    zeros_half = jnp.zeros((seq, half), F32)
    cos_h = jnp.concatenate([cos, cos, ones], axis=1)
    lo_h = jnp.concatenate([-sin, zeros_half, zeros_rest], axis=1)
    hi_h = jnp.concatenate([zeros_half, sin, zeros_rest], axis=1)
    reps = LANES // HEAD_DIM
    return tuple(jnp.tile(t, (1, reps)) for t in (cos_h, lo_h, hi_h))


def _pair_major_heads(n_q_heads):
    order = []
    for p in range(n_q_heads // (2 * GROUP)):
        for j in range(GROUP):
            for gl in range(2):
                order.append((2 * p + gl) * GROUP + j)
    return np.asarray(order, dtype=np.int32)


def kernel(x, c, norm_gain, w_ada, b_ada, w_ffn_in, w_ffn_out, w_hgrn_in, hgrn_lb_logits, hgrn_head_gain,
           w_hgrn_out, kv_gain, w_ada_kv, b_ada_kv, w_kv, b_kv, w_q, b_q, attn_sinks, w_attn_out, final_gain):
    B, S, D = x.shape
    T = B * S
    depth = w_ada.shape[0]
    n_a = w_hgrn_in.shape[0]
    d_ff = w_ffn_out.shape[2]
    ffp = -(-d_ff // 512) * 512
    n_q_heads = w_q.shape[2] // HEAD_DIM

    c_rows = jnp.pad(c, ((0, SUBLANES - B), (0, 0)))
    mod = _ada_mod(c_rows, w_ada, b_ada)[:, :B].reshape(depth, B, N_SUBLAYERS, 3, 1, D)
    mod_kv = _ada_mod(c_rows, w_ada_kv[None], b_ada_kv[None])[0, :B].reshape(B, 2, 1, D)

    col_pad = jnp.zeros((depth, 2, D, ffp - d_ff), BF16)
    w_a_p = jnp.concatenate([w_ffn_in[..., :d_ff].astype(BF16), col_pad], axis=-1)
    w_b_p = jnp.concatenate([w_ffn_in[..., d_ff:].astype(BF16), col_pad], axis=-1)
    w_out_p = jnp.concatenate([w_ffn_out.astype(BF16), jnp.zeros((depth, 2, ffp - d_ff, D), BF16)], axis=2)
    heads = _pair_major_heads(n_q_heads)
    rope = _rope_lane_tables(S)

    h = x.reshape(T, D)
    kv = None
    for layer in range(depth):
        def vecs(sub):
            return (norm_gain[layer, sub][None], mod[layer, :, sub, 0], mod[layer, :, sub, 1], mod[layer, :, sub, 2])

        gain, shift, scale, gate = vecs(0)
        h = _ffn(h, gain, shift, scale, gate, w_a_p[layer, 0], w_b_p[layer, 0], w_out_p[layer, 0], S)

        gain, shift, scale, gate = vecs(1)
        if layer < n_a:
            z = _norm_proj(h, gain, shift, scale, w_hgrn_in[layer].astype(BF16), S, F32, tn=1024)
            o = _hgrn_scan(z, hgrn_lb_logits, hgrn_head_gain[layer][None], B, S, layer)
            h = _proj_res(o, w_hgrn_out[layer].astype(BF16), h, gate, S)
        else:
            bl = layer - n_a
            wq = w_q[bl].reshape(D, n_q_heads, HEAD_DIM)[:, heads].reshape(D, -1).astype(BF16)
            bq = b_q[bl].reshape(n_q_heads, HEAD_DIM)[heads].reshape(1, -1)
            wo = w_attn_out[bl].reshape(n_q_heads, HEAD_DIM, D)[heads].reshape(-1, D).astype(BF16)
            q = _norm_proj(h, gain, shift, scale, wq, S, BF16, bias=bq, rope=rope, rot_cols=wq.shape[1])
            o = _swa(q, kv, attn_sinks[bl], B, S)
            h = _proj_res(o, wo, h, gate, S)

        gain, shift, scale, gate = vecs(2)
        last = layer == depth - 1
        h = _ffn(h, gain, shift, scale, gate, w_a_p[layer, 1], w_b_p[layer, 1], w_out_p[layer, 1], S,
                 final_gain=final_gain[None] if last else None)

        if layer == n_a - 1:
            kvw = w_kv.shape[1]
            kv = _norm_proj(h, kv_gain[None], mod_kv[:, 0], mod_kv[:, 1], w_kv.astype(BF16), S, BF16,
                            bias=b_kv[None], rope=rope, rot_cols=kvw // 2)

    return h.reshape(B, S, D)
```

```python
import functools

import numpy as np
import jax
import jax.numpy as jnp
from jax import lax
from jax.experimental import pallas as pl
from jax.experimental.pallas import tpu as pltpu

F32 = jnp.float32
BF16 = jnp.bfloat16

NORM_EPS = 1e-6
NEG_INF = -1e30
ROPE_THETA = 500000.0

LANES = 128
SUBLANES = 8
MIB = 1 << 20

HGRN_HEAD = 128
HGRN_CHUNK = 128
HEAD_DIM = 64
ROT_DIM = HEAD_DIM // 4
GROUP = 8
WINDOW = 128
N_SUBLAYERS = 3


def _params(semantics, vmem_mib):
    return pltpu.CompilerParams(dimension_semantics=semantics, vmem_limit_bytes=vmem_mib * MIB)


def _sigmoid(x):
    return 1.0 / (1.0 + jnp.exp(-x))


def _ada_norm(h, gain, shift, scale):
    y = h * lax.rsqrt(jnp.mean(h * h, axis=-1, keepdims=True) + NORM_EPS)
    return (y * gain) * (1.0 + scale) + shift


NORM_ROWS = 32
PACK_ROWS = 16


def _ada_norm_into(u_ref, h_ref, gain_ref, shift_ref, scale_ref, inv_scr, zero_ref=None):
    n_rows, d = h_ref.shape
    n_cb = d // LANES

    def pass1(i, carry):
        r = pl.multiple_of(i * SUBLANES, SUBLANES)
        acc = None
        for cb in range(n_cb):
            xb = h_ref[pl.ds(r, SUBLANES), cb * LANES:(cb + 1) * LANES]
            acc = xb * xb if acc is None else acc + xb * xb
        inv_scr[pl.ds(r, SUBLANES), :] = acc
        return carry
    lax.fori_loop(0, n_rows // SUBLANES, pass1, 0, unroll=4)
    ones = jnp.ones((LANES, LANES), BF16)
    ssq = None
    for part in _split_bf16x3(inv_scr[...]):
        s = jnp.dot(part, ones, preferred_element_type=F32)
        ssq = s if ssq is None else ssq + s
    inv_scr[...] = lax.rsqrt(ssq * (1.0 / d) + NORM_EPS)

    mult = gain_ref[...] * (1.0 + scale_ref[...])
    shift = shift_ref[...]

    def pass2(i, carry):
        r = pl.multiple_of(i * PACK_ROWS, PACK_ROWS)
        rows = pl.ds(r, PACK_ROWS)
        inv = inv_scr[rows, :]
        for cb in range(n_cb):
            cols = slice(cb * LANES, (cb + 1) * LANES)
            u_ref[rows, cols] = ((h_ref[rows, cols] * inv) * mult[:, cols] + shift[:, cols]).astype(u_ref.dtype)
            if zero_ref is not None:
                zero_ref[rows, cols] = jnp.zeros((PACK_ROWS, LANES), zero_ref.dtype)
        return carry
    lax.fori_loop(0, n_rows // PACK_ROWS, pass2, 0, unroll=2)


def _mod_kernel(c_ref, w_ref, b_ref, o_ref):
    c = c_ref[...]
    cs = (c * _sigmoid(c)).astype(BF16)
    o_ref[...] = jnp.dot(cs, w_ref[...].astype(BF16), preferred_element_type=F32) + b_ref[...]


def _ada_mod(c_rows, w, b):
    L, D, N = w.shape
    tn = 1024
    return pl.pallas_call(
        _mod_kernel,
        grid=(L, N // tn),
        in_specs=[
            pl.BlockSpec((SUBLANES, D), lambda l, j: (0, 0)),
            pl.BlockSpec((None, D, tn), lambda l, j: (l, 0, j)),
            pl.BlockSpec((None, 1, tn), lambda l, j: (l, 0, j)),
        ],
        out_specs=pl.BlockSpec((None, SUBLANES, tn), lambda l, j: (l, 0, j)),
        out_shape=jax.ShapeDtypeStruct((L, SUBLANES, N), F32),
        compiler_params=_params(("arbitrary", "arbitrary"), 40),
        name="ada_mod",
    )(c_rows, w, b.reshape(L, 1, N))


def _ffn_kernel(*refs, n_f, final_norm):
    if final_norm:
        (h_ref, gain_ref, shift_ref, scale_ref, gate_ref, wa_ref, wb_ref, wo_ref, fg_ref,
         o_ref, u_scr, inv_scr) = refs
    else:
        (h_ref, gain_ref, shift_ref, scale_ref, gate_ref, wa_ref, wb_ref, wo_ref,
         o_ref, u_scr, inv_scr) = refs
    j = pl.program_id(1)

    @pl.when(j == 0)
    def _():
        _ada_norm_into(u_scr, h_ref, gain_ref, shift_ref, scale_ref, inv_scr, zero_ref=o_ref)

    u = u_scr[...]
    a = jnp.dot(u, wa_ref[...], preferred_element_type=F32)
    b = jnp.dot(u, wb_ref[...], preferred_element_type=F32)
    g = ((a * _sigmoid(a)) * b).astype(BF16)
    o_ref[...] += jnp.dot(g, wo_ref[...], preferred_element_type=F32)

    @pl.when(j == n_f - 1)
    def _():
        def body(i, carry):
            r = pl.multiple_of(i * NORM_ROWS, NORM_ROWS)
            rows = pl.ds(r, NORM_ROWS)
            out = h_ref[rows, :] + (0.5 * gate_ref[...]) * o_ref[rows, :]
            if final_norm:
                out = out * lax.rsqrt(jnp.mean(out * out, axis=-1, keepdims=True) + NORM_EPS) * fg_ref[...]
            o_ref[rows, :] = out
            return carry
        lax.fori_loop(0, o_ref.shape[0] // NORM_ROWS, body, 0, unroll=2)


def _ffn(h, gain, shift, scale, gate, w_a_p, w_b_p, w_out_p, seq, final_gain=None, tm=1024, tf=512):
    T, D = h.shape
    Fp = w_out_p.shape[0]
    n_f = Fp // tf
    per_b = seq // tm
    final_norm = final_gain is not None
    vec = pl.BlockSpec((1, D), lambda i, j: (0, 0))
    bvec = pl.BlockSpec((None, 1, D), lambda i, j: (i // per_b, 0, 0))
    in_specs = [
        pl.BlockSpec((tm, D), lambda i, j: (i, 0)),
        vec, bvec, bvec, bvec,
        pl.BlockSpec((D, tf), lambda i, j: (0, j)),
        pl.BlockSpec((D, tf), lambda i, j: (0, j)),
        pl.BlockSpec((tf, D), lambda i, j: (j, 0)),
    ]
    args = [h, gain, shift, scale, gate, w_a_p, w_b_p, w_out_p]
    if final_norm:
        in_specs.append(vec)
        args.append(final_gain)
    return pl.pallas_call(
        functools.partial(_ffn_kernel, n_f=n_f, final_norm=final_norm),
        grid=(T // tm, n_f),
        in_specs=in_specs,
        out_specs=pl.BlockSpec((tm, D), lambda i, j: (i, 0)),
        out_shape=jax.ShapeDtypeStruct((T, D), F32),
        scratch_shapes=[pltpu.VMEM((tm, D), BF16), pltpu.VMEM((tm, LANES), F32)],
        compiler_params=_params(("arbitrary", "arbitrary"), 60),
        name="ffn",
    )(*args)


def _rotate_block(z, cos, sin_lo, sin_hi):
    half = ROT_DIM // 2
    up = pltpu.roll(z, LANES - half, axis=1)
    down = pltpu.roll(z, half, axis=1)
    return z * cos + up * sin_lo + down * sin_hi


def _norm_proj_kernel(*refs, has_bias, rot_cols):
    refs = list(refs)
    h_ref, gain_ref, shift_ref, scale_ref, w_ref = refs[:5]
    pos = 5
    bias_ref = None
    if has_bias:
        bias_ref = refs[pos]
        pos += 1
    if rot_cols:
        cos_ref, slo_ref, shi_ref = refs[pos:pos + 3]
        pos += 3
    o_ref, u_scr, inv_scr = refs[pos:pos + 3]

    @pl.when(pl.program_id(1) == 0)
    def _():
        _ada_norm_into(u_scr, h_ref, gain_ref, shift_ref, scale_ref, inv_scr)

    z = jnp.dot(u_scr[...], w_ref[...], preferred_element_type=F32)
    if has_bias:
        z = z + bias_ref[...]
    if rot_cols:
        cos, slo, shi = cos_ref[...], slo_ref[...], shi_ref[...]
        tn = z.shape[1]
        blocks = []
        for cb in range(tn // LANES):
            zb = z[:, cb * LANES:(cb + 1) * LANES]
            if cb * LANES < rot_cols:
                zb = _rotate_block(zb, cos, slo, shi)
            blocks.append(zb)
        z = jnp.concatenate(blocks, axis=1) if len(blocks) > 1 else blocks[0]
    o_ref[...] = z.astype(o_ref.dtype)


def _norm_proj(h, gain, shift, scale, w, seq, out_dtype, bias=None, rope=None, rot_cols=0, tm=1024, tn=None):
    T, D = h.shape
    N = w.shape[1]
    tn = N if tn is None else tn
    per_b = seq // tm
    vec = pl.BlockSpec((1, D), lambda i, j: (0, 0))
    bvec = pl.BlockSpec((None, 1, D), lambda i, j: (i // per_b, 0, 0))
    in_specs = [pl.BlockSpec((tm, D), lambda i, j: (i, 0)), vec, bvec, bvec,
                pl.BlockSpec((D, tn), lambda i, j: (0, j))]
    args = [h, gain, shift, scale, w]
    if bias is not None:
        in_specs.append(pl.BlockSpec((1, tn), lambda i, j: (0, j)))
        args.append(bias)
    if rot_cols:
        assert tn == N and rot_cols % LANES == 0
        tab = pl.BlockSpec((tm, LANES), lambda i, j: (i % per_b, 0))
        in_specs += [tab, tab, tab]
        args += list(rope)
    return pl.pallas_call(
        functools.partial(_norm_proj_kernel, has_bias=bias is not None, rot_cols=rot_cols),
        grid=(T // tm, N // tn),
        in_specs=in_specs,
        out_specs=pl.BlockSpec((tm, tn), lambda i, j: (i, j)),
        out_shape=jax.ShapeDtypeStruct((T, N), out_dtype),
        scratch_shapes=[pltpu.VMEM((tm, D), BF16), pltpu.VMEM((tm, LANES), F32)],
        compiler_params=_params(("arbitrary", "arbitrary"), 48),
        name="norm_proj",
    )(*args)


def _proj_res_kernel(x_ref, w_ref, h_ref, gate_ref, o_ref):
    y = jnp.dot(x_ref[...], w_ref[...], preferred_element_type=F32)
    o_ref[...] = h_ref[...] + gate_ref[...] * y


def _proj_res(x, w, h, gate, seq, tm=512):
    T, K = x.shape
    N = w.shape[1]
    per_b = seq // tm
    return pl.pallas_call(
        _proj_res_kernel,
        grid=(T // tm,),
        in_specs=[
            pl.BlockSpec((tm, K), lambda i: (i, 0)),
            pl.BlockSpec((K, N), lambda i: (0, 0)),
            pl.BlockSpec((tm, N), lambda i: (i, 0)),
            pl.BlockSpec((None, 1, N), lambda i: (i // per_b, 0, 0)),
        ],
        out_specs=pl.BlockSpec((tm, N), lambda i: (i, 0)),
        out_shape=jax.ShapeDtypeStruct((T, N), F32),
        compiler_params=_params(("arbitrary",), 48),
        name="proj_res",
    )(x, w, h, gate)


def _split_bf16x3(x):
    hi = x.astype(BF16)
    r1 = x - hi.astype(F32)
    mid = r1.astype(BF16)
    lo = (r1 - mid.astype(F32)).astype(BF16)
    return hi, mid, lo


def _hgrn_scan_kernel(zq_ref, zf_ref, zi_ref, zg_ref, lbl_ref, hg_ref, o_ref,
                      st_ref, q_scr, b_scr, c_scr, v_scr, y_scr, bc_scr, cc_scr, mask_scr, spread_scr, tril_scr,
                      *, layer, n_chunks, n_heads):
    C = HGRN_CHUNK
    K = HGRN_HEAD
    levels = []
    m = C // 2
    while m >= SUBLANES:
        levels.append(m)
        m //= 2

    @pl.when(pl.program_id(2) == 0)
    def _():
        st_ref[...] = jnp.zeros_like(st_ref)
        row_c = lax.broadcasted_iota(jnp.int32, (C, C), 0)
        col_c = lax.broadcasted_iota(jnp.int32, (C, C), 1)
        tril = (row_c >= col_c).astype(BF16)
        tril_scr[...] = jnp.concatenate([tril, tril, tril], axis=1)
        xor_c = row_c ^ col_c
        mask_scr[0] = ((xor_c < SUBLANES) & (col_c <= row_c)).astype(F32)
        for li, m in enumerate(levels):
            if li:
                mask_scr[li] = (xor_c < 2 * m).astype(F32)
        src = lax.shift_right_logical(lax.broadcasted_iota(jnp.int32, (SUBLANES * K, C), 0), K.bit_length() - 1)
        dst = lax.broadcasted_iota(jnp.int32, (SUBLANES * K, C), 1) & (SUBLANES - 1)
        spread_scr[...] = (src == dst).astype(BF16)

    def prepare(hh, r0):
        lanes = slice(hh * K, (hh + 1) * K)
        rows = pl.ds(r0, C)
        logits = lbl_ref[:, lanes]
        e = jnp.exp(logits - jnp.max(logits, axis=0, keepdims=True))
        lb = jnp.sum(e[:layer + 1], axis=0, keepdims=True) / jnp.sum(e, axis=0, keepdims=True)

        zq = zq_ref[rows, lanes]
        q_scr[rows, lanes] = zq * _sigmoid(zq)
        forget = lb + (1.0 - lb) * _sigmoid(zf_ref[rows, lanes])
        v_scr[rows, lanes] = zi_ref[rows, lanes].astype(BF16)
        parts = jnp.concatenate(_split_bf16x3(jnp.log2(forget)), axis=0)
        b = jnp.dot(tril_scr[...], parts, preferred_element_type=F32)
        b_scr[rows, lanes] = b
        c_scr[rows, lanes] = b - jnp.log2(1.0 - forget)

    def mix(hh, r0):
        lanes = slice(hh * K, (hh + 1) * K)
        rows = pl.ds(r0, C)
        q = q_scr[rows, lanes]
        b = b_scr[rows, lanes]
        c = c_scr[rows, lanes]
        v16 = v_scr[rows, lanes]
        st = st_ref[hh]
        bc_scr[hh] = b
        cc_scr[hh] = c

        qd = (q * jnp.exp2(b)).astype(BF16)
        out = lax.dot_general(qd, st.astype(BF16), (((1,), (1,)), ((), ())), preferred_element_type=F32)
        b_last = bc_scr[hh, pl.ds(C - 1, 1), :]
        kd = jnp.exp2(b_last - c).astype(BF16)
        st_ref[hh] = st * jnp.exp2(b_last) + lax.dot_general(
            v16, kd, (((0,), (0,)), ((), ())), preferred_element_type=F32)

        w_rows = []
        for blk in range(C // SUBLANES):
            rb = blk * SUBLANES
            qb = q[rb:rb + SUBLANES]
            bb = b[rb:rb + SUBLANES]
            w_s = []
            for s in range(SUBLANES):
                rel = jnp.minimum(bb - cc_scr[hh, pl.ds(rb + s, 1), :], 0.0)
                w_s.append(qb * jnp.exp2(rel))
            w_rows.append(jnp.concatenate(w_s, axis=1))
        w_all = jnp.concatenate(w_rows, axis=0).astype(BF16)
        p = jnp.dot(w_all, spread_scr[...], preferred_element_type=F32) * mask_scr[0]

        n_groups = C // SUBLANES
        zero_group = jnp.zeros((SUBLANES, K), F32)
        for li, m in enumerate(levels):
            qt, kt = [], []
            for g in range(n_groups):
                rg = g * SUBLANES
                rows_g = slice(rg, rg + SUBLANES)
                anchor = (rg // (2 * m)) * 2 * m + m - 1
                ba = bc_scr[hh, pl.ds(anchor, 1), :]
                if rg & m:
                    qt.append(q[rows_g] * jnp.exp2(b[rows_g] - ba))
                    kt.append(zero_group)
                else:
                    qt.append(zero_group)
                    kt.append(jnp.exp2(ba - c[rows_g]))
            a_m = lax.dot_general(jnp.concatenate(qt, axis=0).astype(BF16),
                                  jnp.concatenate(kt, axis=0).astype(BF16),
                                  (((1,), (1,)), ((), ())), preferred_element_type=F32)
            p = p + (a_m if 2 * m == C else a_m * mask_scr[li])
        y_scr[rows, lanes] = out + jnp.dot(p.astype(BF16), v16, preferred_element_type=F32)

    def finish(hh, r0):
        lanes = slice(hh * K, (hh + 1) * K)
        rows = pl.ds(r0, C)
        out = y_scr[rows, lanes]
        y = out * lax.rsqrt(jnp.mean(out * out, axis=-1, keepdims=True) + NORM_EPS) * hg_ref[:, lanes]
        o_ref[rows, lanes] = (y * _sigmoid(zg_ref[rows, lanes])).astype(o_ref.dtype)

    def run(stage, ci):
        r0 = ci * C if isinstance(ci, int) else pl.multiple_of(ci * C, C)
        for hh in range(n_heads):
            stage(hh, r0)

    run(prepare, 0)
    run(mix, 0)
    run(prepare, 1)

    def trip(ci, carry):
        run(finish, ci - 1)
        run(mix, ci)
        run(prepare, jnp.minimum(ci + 1, n_chunks - 1))
        return carry
    lax.fori_loop(1, n_chunks, trip, 0)
    run(finish, n_chunks - 1)


def _hgrn_scan(z, lb_logits, head_gain, batch, seq, layer, ts=1024, hpb=4):
    assert HGRN_CHUNK == HGRN_HEAD
    T, D4 = z.shape
    D = D4 // 4
    n_s = seq // ts
    hw = hpb * HGRN_HEAD
    n_hb = D // hw

    def zspec(group):
        return pl.BlockSpec((ts, hw), lambda bi, hi, si: (bi * n_s + si, group * n_hb + hi))

    n_layers = lb_logits.shape[0]
    n_levels = (HGRN_CHUNK // SUBLANES).bit_length() - 1
    return pl.pallas_call(
        functools.partial(_hgrn_scan_kernel, layer=layer, n_chunks=ts // HGRN_CHUNK, n_heads=hpb),
        grid=(batch, n_hb, n_s),
        in_specs=[zspec(0), zspec(1), zspec(2), zspec(3),
                  pl.BlockSpec((n_layers, hw), lambda bi, hi, si: (0, hi)),
                  pl.BlockSpec((1, hw), lambda bi, hi, si: (0, hi))],
        out_specs=pl.BlockSpec((ts, hw), lambda bi, hi, si: (bi * n_s + si, hi)),
        out_shape=jax.ShapeDtypeStruct((T, D), BF16),
        scratch_shapes=[pltpu.VMEM((hpb, HGRN_HEAD, HGRN_HEAD), F32),
                        pltpu.VMEM((ts, hw), F32),
                        pltpu.VMEM((ts, hw), F32),
                        pltpu.VMEM((ts, hw), F32),
                        pltpu.VMEM((ts, hw), BF16),
                        pltpu.VMEM((ts, hw), F32),
                        pltpu.VMEM((hpb, HGRN_CHUNK, HGRN_HEAD), F32),
                        pltpu.VMEM((hpb, HGRN_CHUNK, HGRN_HEAD), F32),
                        pltpu.VMEM((n_levels, HGRN_CHUNK, HGRN_CHUNK), F32),
                        pltpu.VMEM((SUBLANES * HGRN_HEAD, HGRN_CHUNK), BF16),
                        pltpu.VMEM((HGRN_CHUNK, 3 * HGRN_CHUNK), BF16)],
        compiler_params=_params(("arbitrary", "arbitrary", "arbitrary"), 32),
        name="hgrn_scan",
    )(z, z, z, z, lb_logits, head_gain)


def _swa_kernel(sink_ref, q_ref, kc_ref, kp_ref, vc_ref, vp_ref, o_ref, *, n_blocks):
    W = WINDOW
    pair = pl.program_id(1)
    first_tile = pl.program_id(2) == 0
    lane_k = lax.broadcasted_iota(jnp.int32, (2 * W, LANES), 1)
    qi = lax.broadcasted_iota(jnp.int32, (W, 2 * W), 0)
    kj = lax.broadcasted_iota(jnp.int32, (W, 2 * W), 1)
    in_window = (kj > qi) & (kj <= qi + W)

    for n in range(n_blocks):
        rows = slice(n * W, (n + 1) * W)
        q2 = jnp.concatenate([q_ref[rows, j * LANES:(j + 1) * LANES] for j in range(GROUP)], axis=0)
        if n == 0:
            k_prev, v_prev = kp_ref[...], vp_ref[...]
        else:
            k_prev, v_prev = kc_ref[(n - 1) * W:n * W, :], vc_ref[(n - 1) * W:n * W, :]
        kk = jnp.concatenate([k_prev, kc_ref[rows, :]], axis=0)
        vv = jnp.concatenate([v_prev, vc_ref[rows, :]], axis=0)
        zero = jnp.zeros_like(kk)
        k_bd = jnp.concatenate([jnp.where(lane_k < HEAD_DIM, kk, zero),
                                jnp.where(lane_k >= HEAD_DIM, kk, zero)], axis=0)
        v_bd = jnp.concatenate([jnp.where(lane_k < HEAD_DIM, vv, zero),
                                jnp.where(lane_k >= HEAD_DIM, vv, zero)], axis=0)
        s = lax.dot_general(q2, k_bd, (((1,), (1,)), ((), ())), preferred_element_type=F32)
        s = s * (HEAD_DIM ** -0.5)
        if n == 0:
            mask = in_window & (kj >= jnp.where(first_tile, W, 0))
        else:
            mask = in_window
        p_rows = []
        for j in range(GROUP):
            p_cols = []
            for gl in range(2):
                sub = jnp.where(mask, s[j * W:(j + 1) * W, gl * 2 * W:(gl + 1) * 2 * W], NEG_INF)
                sink = sink_ref[(2 * pair + gl) * GROUP + j]
                mx = jnp.maximum(jnp.max(sub, axis=-1, keepdims=True), sink)
                p = jnp.exp(sub - mx)
                den = jnp.sum(p, axis=-1, keepdims=True) + jnp.exp(sink - mx)
                p_cols.append((p * (1.0 / den)).astype(BF16))
            p_rows.append(jnp.concatenate(p_cols, axis=1))
        probs = jnp.concatenate(p_rows, axis=0)
        out = jnp.dot(probs, v_bd, preferred_element_type=F32)
        for j in range(GROUP):
            o_ref[rows, j * LANES:(j + 1) * LANES] = out[j * W:(j + 1) * W].astype(o_ref.dtype)


def _swa(q, kv, sinks, batch, seq, tq=512):
    T, DQ = q.shape
    n_pairs = kv.shape[1] // (2 * LANES)
    qw = DQ // n_pairs
    n_t = seq // tq
    wpt = tq // WINDOW

    def cur(col0):
        return pl.BlockSpec((tq, LANES), lambda bi, pi, ti: (bi * n_t + ti, col0 + pi))

    def prev(col0):
        return pl.BlockSpec(
            (WINDOW, LANES),
            lambda bi, pi, ti: (jnp.maximum((bi * n_t + ti) * wpt - 1, 0), col0 + pi))

    return pl.pallas_call(
        functools.partial(_swa_kernel, n_blocks=wpt),
        grid=(batch, n_pairs, n_t),
        in_specs=[pl.BlockSpec(memory_space=pltpu.SMEM),
                  pl.BlockSpec((tq, qw), lambda bi, pi, ti: (bi * n_t + ti, pi)),
                  cur(0), prev(0), cur(n_pairs), prev(n_pairs)],
        out_specs=pl.BlockSpec((tq, qw), lambda bi, pi, ti: (bi * n_t + ti, pi)),
        out_shape=jax.ShapeDtypeStruct((T, DQ), BF16),
        compiler_params=_params(("arbitrary", "arbitrary", "arbitrary"), 40),
        name="swa",
    )(sinks, q, kv, kv, kv, kv)


def _rope_lane_tables(seq):
    half = ROT_DIM // 2
    inv_freq = jnp.power(jnp.float32(ROPE_THETA), -jnp.arange(0, ROT_DIM, 2, dtype=F32) / ROT_DIM)
    ang = jnp.arange(seq, dtype=F32)[:, None] * inv_freq[None, :]
    sin, cos = jnp.sin(ang), jnp.cos(ang)
    ones = jnp.ones((seq, HEAD_DIM - ROT_DIM), F32)
    zeros_rest = jnp.zeros((seq, HEAD_DIM - ROT_DIM), F32)
    zeros_half = jnp.zeros((seq, half), F32)
    cos_h = jnp.concatenate([cos, cos, ones], axis=1)
    lo_h = jnp.concatenate([-sin, zeros_half, zeros_rest], axis=1)
    hi_h = jnp.concatenate([zeros_half, sin, zeros_rest], axis=1)
    reps = LANES // HEAD_DIM
    return tuple(jnp.tile(t, (1, reps)) for t in (cos_h, lo_h, hi_h))


def _pair_major_heads(n_q_heads):
    order = []
    for p in range(n_q_heads // (2 * GROUP)):
        for j in range(GROUP):
            for gl in range(2):
                order.append((2 * p + gl) * GROUP + j)
    return np.asarray(order, dtype=np.int32)


def kernel(x, c, norm_gain, w_ada, b_ada, w_ffn_in, w_ffn_out, w_hgrn_in, hgrn_lb_logits, hgrn_head_gain,
           w_hgrn_out, kv_gain, w_ada_kv, b_ada_kv, w_kv, b_kv, w_q, b_q, attn_sinks, w_attn_out, final_gain):
    B, S, D = x.shape
    T = B * S
    depth = w_ada.shape[0]
    n_a = w_hgrn_in.shape[0]
    d_ff = w_ffn_out.shape[2]
    ffp = -(-d_ff // 512) * 512
    n_q_heads = w_q.shape[2] // HEAD_DIM

    c_rows = jnp.pad(c, ((0, SUBLANES - B), (0, 0)))
    mod = _ada_mod(c_rows, w_ada, b_ada)[:, :B].reshape(depth, B, N_SUBLAYERS, 3, 1, D)
    mod_kv = _ada_mod(c_rows, w_ada_kv[None], b_ada_kv[None])[0, :B].reshape(B, 2, 1, D)

    col_pad = jnp.zeros((depth, 2, D, ffp - d_ff), BF16)
    w_a_p = jnp.concatenate([w_ffn_in[..., :d_ff].astype(BF16), col_pad], axis=-1)
    w_b_p = jnp.concatenate([w_ffn_in[..., d_ff:].astype(BF16), col_pad], axis=-1)
    w_out_p = jnp.concatenate([w_ffn_out.astype(BF16), jnp.zeros((depth, 2, ffp - d_ff, D), BF16)], axis=2)
    heads = _pair_major_heads(n_q_heads)
    rope = _rope_lane_tables(S)

    h = x.reshape(T, D)
    kv = None
    for layer in range(depth):
        def vecs(sub):
            return (norm_gain[layer, sub][None], mod[layer, :, sub, 0], mod[layer, :, sub, 1], mod[layer, :, sub, 2])

        gain, shift, scale, gate = vecs(0)
        h = _ffn(h, gain, shift, scale, gate, w_a_p[layer, 0], w_b_p[layer, 0], w_out_p[layer, 0], S)

        gain, shift, scale, gate = vecs(1)
        if layer < n_a:
            z = _norm_proj(h, gain, shift, scale, w_hgrn_in[layer].astype(BF16), S, F32, tn=1024)
            o = _hgrn_scan(z, hgrn_lb_logits, hgrn_head_gain[layer][None], B, S, layer)
            h = _proj_res(o, w_hgrn_out[layer].astype(BF16), h, gate, S)
        else:
            bl = layer - n_a
            wq = w_q[bl].reshape(D, n_q_heads, HEAD_DIM)[:, heads].reshape(D, -1).astype(BF16)
            bq = b_q[bl].reshape(n_q_heads, HEAD_DIM)[heads].reshape(1, -1)
            wo = w_attn_out[bl].reshape(n_q_heads, HEAD_DIM, D)[heads].reshape(-1, D).astype(BF16)
            q = _norm_proj(h, gain, shift, scale, wq, S, BF16, bias=bq, rope=rope, rot_cols=wq.shape[1], tm=512)
            o = _swa(q, kv, attn_sinks[bl], B, S)
            h = _proj_res(o, wo, h, gate, S)

        gain, shift, scale, gate = vecs(2)
        last = layer == depth - 1
        h = _ffn(h, gain, shift, scale, gate, w_a_p[layer, 1], w_b_p[layer, 1], w_out_p[layer, 1], S,
                 final_gain=final_gain[None] if last else None)

        if layer == n_a - 1:
            kvw = w_kv.shape[1]
            kv = _norm_proj(h, kv_gain[None], mod_kv[:, 0], mod_kv[:, 1], w_kv.astype(BF16), S, BF16,
                            bias=b_kv[None], rope=rope, rot_cols=kvw // 2, tm=512)

    return h.reshape(B, S, D)
```

```python
import functools

import numpy as np
import jax
import jax.numpy as jnp
from jax import lax
from jax.experimental import pallas as pl
from jax.experimental.pallas import tpu as pltpu

F32 = jnp.float32
BF16 = jnp.bfloat16

NORM_EPS = 1e-6
NEG_INF = -1e30
ROPE_THETA = 500000.0

LANES = 128
SUBLANES = 8
MIB = 1 << 20

HGRN_HEAD = 128
HGRN_CHUNK = 128
HEAD_DIM = 64
ROT_DIM = HEAD_DIM // 4
GROUP = 8
WINDOW = 128
N_SUBLAYERS = 3


def _params(semantics, vmem_mib):
    return pltpu.CompilerParams(dimension_semantics=semantics, vmem_limit_bytes=vmem_mib * MIB)


def _sigmoid(x):
    return 1.0 / (1.0 + jnp.exp(-x))


def _ada_norm(h, gain, shift, scale):
    y = h * lax.rsqrt(jnp.mean(h * h, axis=-1, keepdims=True) + NORM_EPS)
    return (y * gain) * (1.0 + scale) + shift


NORM_ROWS = 32
PACK_ROWS = 16


def _ada_norm_into(u_ref, h_ref, gain_ref, shift_ref, scale_ref, inv_scr, zero_ref=None):
    n_rows, d = h_ref.shape
    n_cb = d // LANES

    def pass1(i, carry):
        r = pl.multiple_of(i * SUBLANES, SUBLANES)
        acc = None
        for cb in range(n_cb):
            xb = h_ref[pl.ds(r, SUBLANES), cb * LANES:(cb + 1) * LANES]
            acc = xb * xb if acc is None else acc + xb * xb
        inv_scr[pl.ds(r, SUBLANES), :] = acc
        return carry
    lax.fori_loop(0, n_rows // SUBLANES, pass1, 0, unroll=4)
    ones = jnp.ones((LANES, LANES), BF16)
    ssq = None
    for part in _split_bf16x3(inv_scr[...]):
        s = jnp.dot(part, ones, preferred_element_type=F32)
        ssq = s if ssq is None else ssq + s
    inv_scr[...] = lax.rsqrt(ssq * (1.0 / d) + NORM_EPS)

    mult = gain_ref[...] * (1.0 + scale_ref[...])
    shift = shift_ref[...]

    def pass2(i, carry):
        r = pl.multiple_of(i * PACK_ROWS, PACK_ROWS)
        rows = pl.ds(r, PACK_ROWS)
        inv = inv_scr[rows, :]
        for cb in range(n_cb):
            cols = slice(cb * LANES, (cb + 1) * LANES)
            u_ref[rows, cols] = ((h_ref[rows, cols] * inv) * mult[:, cols] + shift[:, cols]).astype(u_ref.dtype)
            if zero_ref is not None:
                zero_ref[rows, cols] = jnp.zeros((PACK_ROWS, LANES), zero_ref.dtype)
        return carry
    lax.fori_loop(0, n_rows // PACK_ROWS, pass2, 0, unroll=2)


def _mod_kernel(c_ref, w_ref, b_ref, o_ref):
    c = c_ref[...]
    cs = (c * _sigmoid(c)).astype(BF16)
    o_ref[...] = jnp.dot(cs, w_ref[...].astype(BF16), preferred_element_type=F32) + b_ref[...]


def _ada_mod(c_rows, w, b):
    L, D, N = w.shape
    tn = 1024
    return pl.pallas_call(
        _mod_kernel,
        grid=(L, N // tn),
        in_specs=[
            pl.BlockSpec((SUBLANES, D), lambda l, j: (0, 0)),
            pl.BlockSpec((None, D, tn), lambda l, j: (l, 0, j)),
            pl.BlockSpec((None, 1, tn), lambda l, j: (l, 0, j)),
        ],
        out_specs=pl.BlockSpec((None, SUBLANES, tn), lambda l, j: (l, 0, j)),
        out_shape=jax.ShapeDtypeStruct((L, SUBLANES, N), F32),
        compiler_params=_params(("arbitrary", "arbitrary"), 40),
        name="ada_mod",
    )(c_rows, w, b.reshape(L, 1, N))


def _ffn_kernel(*refs, n_f, tail, final_norm):
    if final_norm:
        (h_ref, gain_ref, shift_ref, scale_ref, gate_ref, wa_ref, wb_ref, wo_ref, fg_ref,
         o_ref, u_scr, inv_scr) = refs
    else:
        (h_ref, gain_ref, shift_ref, scale_ref, gate_ref, wa_ref, wb_ref, wo_ref,
         o_ref, u_scr, inv_scr) = refs
    j = pl.program_id(1)

    @pl.when(j == 0)
    def _():
        _ada_norm_into(u_scr, h_ref, gain_ref, shift_ref, scale_ref, inv_scr, zero_ref=o_ref)

    def accumulate(cols):
        u = u_scr[...]
        a = jnp.dot(u, wa_ref[:, :cols], preferred_element_type=F32)
        b = jnp.dot(u, wb_ref[:, :cols], preferred_element_type=F32)
        g = ((a * _sigmoid(a)) * b).astype(BF16)
        o_ref[...] += jnp.dot(g, wo_ref[:cols, :], preferred_element_type=F32)

    tf = wa_ref.shape[1]
    if tail == tf:
        accumulate(tf)
    else:
        pl.when(j < n_f - 1)(lambda: accumulate(tf))
        pl.when(j == n_f - 1)(lambda: accumulate(tail))

    @pl.when(j == n_f - 1)
    def _():
        def body(i, carry):
            r = pl.multiple_of(i * NORM_ROWS, NORM_ROWS)
            rows = pl.ds(r, NORM_ROWS)
            out = h_ref[rows, :] + (0.5 * gate_ref[...]) * o_ref[rows, :]
            if final_norm:
                out = out * lax.rsqrt(jnp.mean(out * out, axis=-1, keepdims=True) + NORM_EPS) * fg_ref[...]
            o_ref[rows, :] = out
            return carry
        lax.fori_loop(0, o_ref.shape[0] // NORM_ROWS, body, 0, unroll=2)


def _ffn(h, gain, shift, scale, gate, w_a, w_b, w_out, seq, final_gain=None, tm=1024, tf=512):
    T, D = h.shape
    d_ff = w_out.shape[0]
    n_f = pl.cdiv(d_ff, tf)
    tail = d_ff - (n_f - 1) * tf
    assert tail % LANES == 0
    per_b = seq // tm
    final_norm = final_gain is not None
    vec = pl.BlockSpec((1, D), lambda i, j: (0, 0))
    bvec = pl.BlockSpec((None, 1, D), lambda i, j: (i // per_b, 0, 0))
    in_specs = [
        pl.BlockSpec((tm, D), lambda i, j: (i, 0)),
        vec, bvec, bvec, bvec,
        pl.BlockSpec((D, tf), lambda i, j: (0, j)),
        pl.BlockSpec((D, tf), lambda i, j: (0, j)),
        pl.BlockSpec((tf, D), lambda i, j: (j, 0)),
    ]
    args = [h, gain, shift, scale, gate, w_a, w_b, w_out]
    if final_norm:
        in_specs.append(vec)
        args.append(final_gain)
    return pl.pallas_call(
        functools.partial(_ffn_kernel, n_f=n_f, tail=tail, final_norm=final_norm),
        grid=(T // tm, n_f),
        in_specs=in_specs,
        out_specs=pl.BlockSpec((tm, D), lambda i, j: (i, 0)),
        out_shape=jax.ShapeDtypeStruct((T, D), F32),
        scratch_shapes=[pltpu.VMEM((tm, D), BF16), pltpu.VMEM((tm, LANES), F32)],
        compiler_params=_params(("arbitrary", "arbitrary"), 60),
        name="ffn",
    )(*args)


def _rotate_block(z, cos, sin_lo, sin_hi):
    half = ROT_DIM // 2
    up = pltpu.roll(z, LANES - half, axis=1)
    down = pltpu.roll(z, half, axis=1)
    return z * cos + up * sin_lo + down * sin_hi


def _norm_proj_kernel(*refs, has_bias, rot_cols):
    refs = list(refs)
    h_ref, gain_ref, shift_ref, scale_ref, w_ref = refs[:5]
    pos = 5
    bias_ref = None
    if has_bias:
        bias_ref = refs[pos]
        pos += 1
    if rot_cols:
        cos_ref, slo_ref, shi_ref = refs[pos:pos + 3]
        pos += 3
    o_ref, u_scr, inv_scr = refs[pos:pos + 3]

    @pl.when(pl.program_id(1) == 0)
    def _():
        _ada_norm_into(u_scr, h_ref, gain_ref, shift_ref, scale_ref, inv_scr)

    z = jnp.dot(u_scr[...], w_ref[...], preferred_element_type=F32)
    if has_bias:
        z = z + bias_ref[...]
    if rot_cols:
        cos, slo, shi = cos_ref[...], slo_ref[...], shi_ref[...]
        tn = z.shape[1]
        blocks = []
        for cb in range(tn // LANES):
            zb = z[:, cb * LANES:(cb + 1) * LANES]
            if cb * LANES < rot_cols:
                zb = _rotate_block(zb, cos, slo, shi)
            blocks.append(zb)
        z = jnp.concatenate(blocks, axis=1) if len(blocks) > 1 else blocks[0]
    o_ref[...] = z.astype(o_ref.dtype)


def _norm_proj(h, gain, shift, scale, w, seq, out_dtype, bias=None, rope=None, rot_cols=0, tm=1024, tn=None):
    T, D = h.shape
    N = w.shape[1]
    tn = N if tn is None else tn
    per_b = seq // tm
    vec = pl.BlockSpec((1, D), lambda i, j: (0, 0))
    bvec = pl.BlockSpec((None, 1, D), lambda i, j: (i // per_b, 0, 0))
    in_specs = [pl.BlockSpec((tm, D), lambda i, j: (i, 0)), vec, bvec, bvec,
                pl.BlockSpec((D, tn), lambda i, j: (0, j))]
    args = [h, gain, shift, scale, w]
    if bias is not None:
        in_specs.append(pl.BlockSpec((1, tn), lambda i, j: (0, j)))
        args.append(bias)
    if rot_cols:
        assert tn == N and rot_cols % LANES == 0
        tab = pl.BlockSpec((tm, LANES), lambda i, j: (i % per_b, 0))
        in_specs += [tab, tab, tab]
        args += list(rope)
    return pl.pallas_call(
        functools.partial(_norm_proj_kernel, has_bias=bias is not None, rot_cols=rot_cols),
        grid=(T // tm, N // tn),
        in_specs=in_specs,
        out_specs=pl.BlockSpec((tm, tn), lambda i, j: (i, j)),
        out_shape=jax.ShapeDtypeStruct((T, N), out_dtype),
        scratch_shapes=[pltpu.VMEM((tm, D), BF16), pltpu.VMEM((tm, LANES), F32)],
        compiler_params=_params(("arbitrary", "arbitrary"), 48),
        name="norm_proj",
    )(*args)


def _proj_res_kernel(x_ref, w_ref, h_ref, gate_ref, o_ref):
    y = jnp.dot(x_ref[...], w_ref[...], preferred_element_type=F32)
    o_ref[...] = h_ref[...] + gate_ref[...] * y


def _proj_res(x, w, h, gate, seq, tm=512):
    T, K = x.shape
    N = w.shape[1]
    per_b = seq // tm
    return pl.pallas_call(
        _proj_res_kernel,
        grid=(T // tm,),
        in_specs=[
            pl.BlockSpec((tm, K), lambda i: (i, 0)),
            pl.BlockSpec((K, N), lambda i: (0, 0)),
            pl.BlockSpec((tm, N), lambda i: (i, 0)),
            pl.BlockSpec((None, 1, N), lambda i: (i // per_b, 0, 0)),
        ],
        out_specs=pl.BlockSpec((tm, N), lambda i: (i, 0)),
        out_shape=jax.ShapeDtypeStruct((T, N), F32),
        compiler_params=_params(("arbitrary",), 48),
        name="proj_res",
    )(x, w, h, gate)


def _split_bf16x3(x):
    hi = x.astype(BF16)
    r1 = x - hi.astype(F32)
    mid = r1.astype(BF16)
    lo = (r1 - mid.astype(F32)).astype(BF16)
    return hi, mid, lo


def _hgrn_scan_kernel(zq_ref, zf_ref, zi_ref, zg_ref, lbl_ref, hg_ref, o_ref,
                      st_ref, q_scr, b_scr, c_scr, v_scr, y_scr, bc_scr, cc_scr, mask_scr, spread_scr, tril_scr,
                      *, layer, n_chunks, n_heads):
    C = HGRN_CHUNK
    K = HGRN_HEAD
    levels = []
    m = C // 2
    while m >= SUBLANES:
        levels.append(m)
        m //= 2

    @pl.when(pl.program_id(2) == 0)
    def _():
        st_ref[...] = jnp.zeros_like(st_ref)
        row_c = lax.broadcasted_iota(jnp.int32, (C, C), 0)
        col_c = lax.broadcasted_iota(jnp.int32, (C, C), 1)
        tril = (row_c >= col_c).astype(BF16)
        tril_scr[...] = jnp.concatenate([tril, tril, tril], axis=1)
        xor_c = row_c ^ col_c
        mask_scr[0] = ((xor_c < SUBLANES) & (col_c <= row_c)).astype(F32)
        for li, m in enumerate(levels):
            if li:
                mask_scr[li] = (xor_c < 2 * m).astype(F32)
        src = lax.shift_right_logical(lax.broadcasted_iota(jnp.int32, (SUBLANES * K, C), 0), K.bit_length() - 1)
        dst = lax.broadcasted_iota(jnp.int32, (SUBLANES * K, C), 1) & (SUBLANES - 1)
        spread_scr[...] = (src == dst).astype(BF16)

    def prepare(hh, r0):
        lanes = slice(hh * K, (hh + 1) * K)
        rows = pl.ds(r0, C)
        logits = lbl_ref[:, lanes]
        e = jnp.exp(logits - jnp.max(logits, axis=0, keepdims=True))
        lb = jnp.sum(e[:layer + 1], axis=0, keepdims=True) / jnp.sum(e, axis=0, keepdims=True)

        zq = zq_ref[rows, lanes]
        q_scr[rows, lanes] = zq * _sigmoid(zq)
        forget = lb + (1.0 - lb) * _sigmoid(zf_ref[rows, lanes])
        v_scr[rows, lanes] = zi_ref[rows, lanes].astype(BF16)
        parts = jnp.concatenate(_split_bf16x3(jnp.log2(forget)), axis=0)
        b = jnp.dot(tril_scr[...], parts, preferred_element_type=F32)
        b_scr[rows, lanes] = b
        c_scr[rows, lanes] = b - jnp.log2(1.0 - forget)

    def mix(hh, r0):
        lanes = slice(hh * K, (hh + 1) * K)
        rows = pl.ds(r0, C)
        q = q_scr[rows, lanes]
        b = b_scr[rows, lanes]
        c = c_scr[rows, lanes]
        v16 = v_scr[rows, lanes]
        st = st_ref[hh]
        bc_scr[hh] = b
        cc_scr[hh] = c

        qd = (q * jnp.exp2(b)).astype(BF16)
        out = lax.dot_general(qd, st.astype(BF16), (((1,), (1,)), ((), ())), preferred_element_type=F32)
        b_last = bc_scr[hh, pl.ds(C - 1, 1), :]
        kd = jnp.exp2(b_last - c).astype(BF16)
        st_ref[hh] = st * jnp.exp2(b_last) + lax.dot_general(
            v16, kd, (((0,), (0,)), ((), ())), preferred_element_type=F32)

        w_rows = []
        for blk in range(C // SUBLANES):
            rb = blk * SUBLANES
            qb = q[rb:rb + SUBLANES]
            bb = b[rb:rb + SUBLANES]
            w_s = []
            for s in range(SUBLANES):
                rel = jnp.minimum(bb - cc_scr[hh, pl.ds(rb + s, 1), :], 0.0)
                w_s.append(qb * jnp.exp2(rel))
            w_rows.append(jnp.concatenate(w_s, axis=1))
        w_all = jnp.concatenate(w_rows, axis=0).astype(BF16)
        p = jnp.dot(w_all, spread_scr[...], preferred_element_type=F32) * mask_scr[0]

        n_groups = C // SUBLANES
        zero_group = jnp.zeros((SUBLANES, K), F32)
        for li, m in enumerate(levels):
            qt, kt = [], []
            for g in range(n_groups):
                rg = g * SUBLANES
                rows_g = slice(rg, rg + SUBLANES)
                anchor = (rg // (2 * m)) * 2 * m + m - 1
                ba = bc_scr[hh, pl.ds(anchor, 1), :]
                if rg & m:
                    qt.append(q[rows_g] * jnp.exp2(b[rows_g] - ba))
                    kt.append(zero_group)
                else:
                    qt.append(zero_group)
                    kt.append(jnp.exp2(ba - c[rows_g]))
            a_m = lax.dot_general(jnp.concatenate(qt, axis=0).astype(BF16),
                                  jnp.concatenate(kt, axis=0).astype(BF16),
                                  (((1,), (1,)), ((), ())), preferred_element_type=F32)
            p = p + (a_m if 2 * m == C else a_m * mask_scr[li])
        y_scr[rows, lanes] = out + jnp.dot(p.astype(BF16), v16, preferred_element_type=F32)

    def finish(hh, r0):
        lanes = slice(hh * K, (hh + 1) * K)
        rows = pl.ds(r0, C)
        out = y_scr[rows, lanes]
        y = out * lax.rsqrt(jnp.mean(out * out, axis=-1, keepdims=True) + NORM_EPS) * hg_ref[:, lanes]
        o_ref[rows, lanes] = (y * _sigmoid(zg_ref[rows, lanes])).astype(o_ref.dtype)

    def run(stage, ci):
        r0 = ci * C if isinstance(ci, int) else pl.multiple_of(ci * C, C)
        for hh in range(n_heads):
            stage(hh, r0)

    run(prepare, 0)
    run(mix, 0)
    run(prepare, 1)

    def trip(ci, carry):
        run(finish, ci - 1)
        run(mix, ci)
        run(prepare, jnp.minimum(ci + 1, n_chunks - 1))
        return carry
    lax.fori_loop(1, n_chunks, trip, 0)
    run(finish, n_chunks - 1)


def _hgrn_scan(z, lb_logits, head_gain, batch, seq, layer, ts=1024, hpb=4):
    assert HGRN_CHUNK == HGRN_HEAD
    T, D4 = z.shape
    D = D4 // 4
    n_s = seq // ts
    hw = hpb * HGRN_HEAD
    n_hb = D // hw

    def zspec(group):
        return pl.BlockSpec((ts, hw), lambda bi, hi, si: (bi * n_s + si, group * n_hb + hi))

    n_layers = lb_logits.shape[0]
    n_levels = (HGRN_CHUNK // SUBLANES).bit_length() - 1
    return pl.pallas_call(
        functools.partial(_hgrn_scan_kernel, layer=layer, n_chunks=ts // HGRN_CHUNK, n_heads=hpb),
        grid=(batch, n_hb, n_s),
        in_specs=[zspec(0), zspec(1), zspec(2), zspec(3),
                  pl.BlockSpec((n_layers, hw), lambda bi, hi, si: (0, hi)),
                  pl.BlockSpec((1, hw), lambda bi, hi, si: (0, hi))],
        out_specs=pl.BlockSpec((ts, hw), lambda bi, hi, si: (bi * n_s + si, hi)),
        out_shape=jax.ShapeDtypeStruct((T, D), BF16),
        scratch_shapes=[pltpu.VMEM((hpb, HGRN_HEAD, HGRN_HEAD), F32),
                        pltpu.VMEM((ts, hw), F32),
                        pltpu.VMEM((ts, hw), F32),
                        pltpu.VMEM((ts, hw), F32),
                        pltpu.VMEM((ts, hw), BF16),
                        pltpu.VMEM((ts, hw), F32),
                        pltpu.VMEM((hpb, HGRN_CHUNK, HGRN_HEAD), F32),
                        pltpu.VMEM((hpb, HGRN_CHUNK, HGRN_HEAD), F32),
                        pltpu.VMEM((n_levels, HGRN_CHUNK, HGRN_CHUNK), F32),
                        pltpu.VMEM((SUBLANES * HGRN_HEAD, HGRN_CHUNK), BF16),
                        pltpu.VMEM((HGRN_CHUNK, 3 * HGRN_CHUNK), BF16)],
        compiler_params=_params(("arbitrary", "arbitrary", "arbitrary"), 32),
        name="hgrn_scan",
    )(z, z, z, z, lb_logits, head_gain)


def _swa_kernel(sink_ref, q_ref, kc_ref, kp_ref, vc_ref, vp_ref, o_ref, bias_scr, *, n_blocks):
    W = WINDOW
    pair = pl.program_id(1)
    first_tile = pl.program_id(2) == 0
    lane_k = lax.broadcasted_iota(jnp.int32, (2 * W, LANES), 1)
    qi = lax.broadcasted_iota(jnp.int32, (W, 2 * W), 0)
    kj = lax.broadcasted_iota(jnp.int32, (W, 2 * W), 1)
    in_window = (kj > qi) & (kj <= qi + W)
    bias_scr[0] = jnp.where(in_window & (kj >= jnp.where(first_tile, W, 0)), 0.0, NEG_INF)
    bias_scr[1] = jnp.where(in_window, 0.0, NEG_INF)

    for n in range(n_blocks):
        rows = slice(n * W, (n + 1) * W)
        q2 = jnp.concatenate([q_ref[rows, j * LANES:(j + 1) * LANES] for j in range(GROUP)], axis=0)
        q2 = q2 * jnp.asarray(HEAD_DIM ** -0.5, q2.dtype)
        if n == 0:
            k_prev, v_prev = kp_ref[...], vp_ref[...]
        else:
            k_prev, v_prev = kc_ref[(n - 1) * W:n * W, :], vc_ref[(n - 1) * W:n * W, :]
        kk = jnp.concatenate([k_prev, kc_ref[rows, :]], axis=0)
        vv = jnp.concatenate([v_prev, vc_ref[rows, :]], axis=0)
        zero = jnp.zeros_like(kk)
        k_bd = jnp.concatenate([jnp.where(lane_k < HEAD_DIM, kk, zero),
                                jnp.where(lane_k >= HEAD_DIM, kk, zero)], axis=0)
        v_bd = jnp.concatenate([jnp.where(lane_k < HEAD_DIM, vv, zero),
                                jnp.where(lane_k >= HEAD_DIM, vv, zero)], axis=0)
        s = lax.dot_general(q2, k_bd, (((1,), (1,)), ((), ())), preferred_element_type=F32)
        bias = bias_scr[0 if n == 0 else 1]
        p_rows = []
        for j in range(GROUP):
            p_cols = []
            for gl in range(2):
                sub = s[j * W:(j + 1) * W, gl * 2 * W:(gl + 1) * 2 * W] + bias
                sink = sink_ref[(2 * pair + gl) * GROUP + j]
                mx = jnp.maximum(jnp.max(sub, axis=-1, keepdims=True), sink)
                p = jnp.exp(sub - mx)
                den = jnp.sum(p, axis=-1, keepdims=True) + jnp.exp(sink - mx)
                p_cols.append((p * (1.0 / den)).astype(BF16))
            p_rows.append(jnp.concatenate(p_cols, axis=1))
        probs = jnp.concatenate(p_rows, axis=0)
        out = jnp.dot(probs, v_bd, preferred_element_type=F32)
        for j in range(GROUP):
            o_ref[rows, j * LANES:(j + 1) * LANES] = out[j * W:(j + 1) * W].astype(o_ref.dtype)


def _swa(q, kv, sinks, batch, seq, tq=512):
    T, DQ = q.shape
    n_pairs = kv.shape[1] // (2 * LANES)
    qw = DQ // n_pairs
    n_t = seq // tq
    wpt = tq // WINDOW

    def cur(col0):
        return pl.BlockSpec((tq, LANES), lambda bi, pi, ti: (bi * n_t + ti, col0 + pi))

    def prev(col0):
        return pl.BlockSpec(
            (WINDOW, LANES),
            lambda bi, pi, ti: (jnp.maximum((bi * n_t + ti) * wpt - 1, 0), col0 + pi))

    return pl.pallas_call(
        functools.partial(_swa_kernel, n_blocks=wpt),
        grid=(batch, n_pairs, n_t),
        in_specs=[pl.BlockSpec(memory_space=pltpu.SMEM),
                  pl.BlockSpec((tq, qw), lambda bi, pi, ti: (bi * n_t + ti, pi)),
                  cur(0), prev(0), cur(n_pairs), prev(n_pairs)],
        out_specs=pl.BlockSpec((tq, qw), lambda bi, pi, ti: (bi * n_t + ti, pi)),
        out_shape=jax.ShapeDtypeStruct((T, DQ), BF16),
        scratch_shapes=[pltpu.VMEM((2, WINDOW, 2 * WINDOW), F32)],
        compiler_params=_params(("arbitrary", "arbitrary", "arbitrary"), 40),
        name="swa",
    )(sinks, q, kv, kv, kv, kv)


def _rope_lane_tables(seq):
    half = ROT_DIM // 2
    inv_freq = jnp.power(jnp.float32(ROPE_THETA), -jnp.arange(0, ROT_DIM, 2, dtype=F32) / ROT_DIM)
    ang = jnp.arange(seq, dtype=F32)[:, None] * inv_freq[None, :]
    sin, cos = jnp.sin(ang), jnp.cos(ang)
    ones = jnp.ones((seq, HEAD_DIM - ROT_DIM), F32)
    zeros_rest = jnp.zeros((seq, HEAD_DIM - ROT_DIM), F32)
    zeros_half = jnp.zeros((seq, half), F32)
    cos_h = jnp.concatenate([cos, cos, ones], axis=1)
    lo_h = jnp.concatenate([-sin, zeros_half, zeros_rest], axis=1)
    hi_h = jnp.concatenate([zeros_half, sin, zeros_rest], axis=1)
    reps = LANES // HEAD_DIM
    return tuple(jnp.tile(t, (1, reps)) for t in (cos_h, lo_h, hi_h))


def _pair_major_heads(n_q_heads):
    order = []
    for p in range(n_q_heads // (2 * GROUP)):
        for j in range(GROUP):
            for gl in range(2):
                order.append((2 * p + gl) * GROUP + j)
    return np.asarray(order, dtype=np.int32)


def kernel(x, c, norm_gain, w_ada, b_ada, w_ffn_in, w_ffn_out, w_hgrn_in, hgrn_lb_logits, hgrn_head_gain,
           w_hgrn_out, kv_gain, w_ada_kv, b_ada_kv, w_kv, b_kv, w_q, b_q, attn_sinks, w_attn_out, final_gain):
    B, S, D = x.shape
    T = B * S
    depth = w_ada.shape[0]
    n_a = w_hgrn_in.shape[0]
    d_ff = w_ffn_out.shape[2]
    n_q_heads = w_q.shape[2] // HEAD_DIM

    c_rows = jnp.pad(c, ((0, SUBLANES - B), (0, 0)))
    mod = _ada_mod(c_rows, w_ada, b_ada)[:, :B].reshape(depth, B, N_SUBLAYERS, 3, 1, D)
    mod_kv = _ada_mod(c_rows, w_ada_kv[None], b_ada_kv[None])[0, :B].reshape(B, 2, 1, D)

    w_a = w_ffn_in[..., :d_ff].astype(BF16)
    w_b = w_ffn_in[..., d_ff:].astype(BF16)
    w_out = w_ffn_out.astype(BF16)
    heads = _pair_major_heads(n_q_heads)
    rope = _rope_lane_tables(S)

    h = x.reshape(T, D)
    kv = None
    for layer in range(depth):
        def vecs(sub):
            return (norm_gain[layer, sub][None], mod[layer, :, sub, 0], mod[layer, :, sub, 1], mod[layer, :, sub, 2])

        gain, shift, scale, gate = vecs(0)
        h = _ffn(h, gain, shift, scale, gate, w_a[layer, 0], w_b[layer, 0], w_out[layer, 0], S)

        gain, shift, scale, gate = vecs(1)
        if layer < n_a:
            z = _norm_proj(h, gain, shift, scale, w_hgrn_in[layer].astype(BF16), S, F32, tn=1024)
            o = _hgrn_scan(z, hgrn_lb_logits, hgrn_head_gain[layer][None], B, S, layer)
            h = _proj_res(o, w_hgrn_out[layer].astype(BF16), h, gate, S)
        else:
            bl = layer - n_a
            wq = w_q[bl].reshape(D, n_q_heads, HEAD_DIM)[:, heads].reshape(D, -1).astype(BF16)
            bq = b_q[bl].reshape(n_q_heads, HEAD_DIM)[heads].reshape(1, -1)
            wo = w_attn_out[bl].reshape(n_q_heads, HEAD_DIM, D)[heads].reshape(-1, D).astype(BF16)
            q = _norm_proj(h, gain, shift, scale, wq, S, BF16, bias=bq, rope=rope, rot_cols=wq.shape[1], tm=512)
            o = _swa(q, kv, attn_sinks[bl], B, S)
            h = _proj_res(o, wo, h, gate, S)

        gain, shift, scale, gate = vecs(2)
        last = layer == depth - 1
        h = _ffn(h, gain, shift, scale, gate, w_a[layer, 1], w_b[layer, 1], w_out[layer, 1], S,
                 final_gain=final_gain[None] if last else None)

        if layer == n_a - 1:
            kvw = w_kv.shape[1]
            kv = _norm_proj(h, kv_gain[None], mod_kv[:, 0], mod_kv[:, 1], w_kv.astype(BF16), S, BF16,
                            bias=b_kv[None], rope=rope, rot_cols=kvw // 2, tm=512)

    return h.reshape(B, S, D)
```

```python
import functools

import numpy as np
import jax
import jax.numpy as jnp
from jax import lax
from jax.experimental import pallas as pl
from jax.experimental.pallas import tpu as pltpu

F32 = jnp.float32
BF16 = jnp.bfloat16

NORM_EPS = 1e-6
NEG_INF = -1e30
ROPE_THETA = 500000.0

LANES = 128
SUBLANES = 8
MIB = 1 << 20

HGRN_HEAD = 128
HGRN_CHUNK = 128
HEAD_DIM = 64
ROT_DIM = HEAD_DIM // 4
GROUP = 8
WINDOW = 128
N_SUBLAYERS = 3


def _params(semantics, vmem_mib):
    return pltpu.CompilerParams(dimension_semantics=semantics, vmem_limit_bytes=vmem_mib * MIB)


def _sigmoid(x):
    return 1.0 / (1.0 + jnp.exp(-x))


def _ada_norm(h, gain, shift, scale):
    y = h * lax.rsqrt(jnp.mean(h * h, axis=-1, keepdims=True) + NORM_EPS)
    return (y * gain) * (1.0 + scale) + shift


NORM_ROWS = 32
PACK_ROWS = 16


def _ada_norm_into(u_ref, h_ref, gain_ref, shift_ref, scale_ref, inv_scr, zero_ref=None):
    n_rows, d = h_ref.shape
    n_cb = d // LANES

    def pass1(i, carry):
        r = pl.multiple_of(i * SUBLANES, SUBLANES)
        acc = None
        for cb in range(n_cb):
            xb = h_ref[pl.ds(r, SUBLANES), cb * LANES:(cb + 1) * LANES]
            acc = xb * xb if acc is None else acc + xb * xb
        inv_scr[pl.ds(r, SUBLANES), :] = acc
        return carry
    lax.fori_loop(0, n_rows // SUBLANES, pass1, 0, unroll=4)
    ones = jnp.ones((LANES, LANES), BF16)
    ssq = None
    for part in _split_bf16x3(inv_scr[...]):
        s = jnp.dot(part, ones, preferred_element_type=F32)
        ssq = s if ssq is None else ssq + s
    inv_scr[...] = lax.rsqrt(ssq * (1.0 / d) + NORM_EPS)

    mult = gain_ref[...] * (1.0 + scale_ref[...])
    shift = shift_ref[...]

    def pass2(i, carry):
        r = pl.multiple_of(i * PACK_ROWS, PACK_ROWS)
        rows = pl.ds(r, PACK_ROWS)
        inv = inv_scr[rows, :]
        for cb in range(n_cb):
            cols = slice(cb * LANES, (cb + 1) * LANES)
            u_ref[rows, cols] = ((h_ref[rows, cols] * inv) * mult[:, cols] + shift[:, cols]).astype(u_ref.dtype)
            if zero_ref is not None:
                zero_ref[rows, cols] = jnp.zeros((PACK_ROWS, LANES), zero_ref.dtype)
        return carry
    lax.fori_loop(0, n_rows // PACK_ROWS, pass2, 0, unroll=2)


def _mod_kernel(c_ref, w_ref, b_ref, o_ref):
    c = c_ref[...]
    cs = (c * _sigmoid(c)).astype(BF16)
    o_ref[...] = jnp.dot(cs, w_ref[...].astype(BF16), preferred_element_type=F32) + b_ref[...]


def _ada_mod(c_rows, w, b):
    L, D, N = w.shape
    tn = 1024
    return pl.pallas_call(
        _mod_kernel,
        grid=(L, N // tn),
        in_specs=[
            pl.BlockSpec((SUBLANES, D), lambda l, j: (0, 0)),
            pl.BlockSpec((None, D, tn), lambda l, j: (l, 0, j)),
            pl.BlockSpec((None, 1, tn), lambda l, j: (l, 0, j)),
        ],
        out_specs=pl.BlockSpec((None, SUBLANES, tn), lambda l, j: (l, 0, j)),
        out_shape=jax.ShapeDtypeStruct((L, SUBLANES, N), F32),
        compiler_params=_params(("arbitrary", "arbitrary"), 40),
        name="ada_mod",
    )(c_rows, w, b.reshape(L, 1, N))


def _ffn_kernel(*refs, n_f, tail, final_norm, n_cast):
    refs = list(refs)
    h_ref, gain_ref, shift_ref, scale_ref, gate_ref, wa_ref, wb_ref, wo_ref = refs[:8]
    pos = 8
    fg_ref = None
    if final_norm:
        fg_ref = refs[pos]
        pos += 1
    cast_src = refs[pos:pos + n_cast]
    o_ref = refs[pos + n_cast]
    cast_dst = refs[pos + n_cast + 1:pos + 2 * n_cast + 1]
    u_scr, inv_scr = refs[pos + 2 * n_cast + 1:]
    j = pl.program_id(1)

    @pl.when(j == 0)
    def _():
        _ada_norm_into(u_scr, h_ref, gain_ref, shift_ref, scale_ref, inv_scr, zero_ref=o_ref)

    def accumulate(cols):
        u = u_scr[...]
        a = jnp.dot(u, wa_ref[:, :cols], preferred_element_type=F32)
        b = jnp.dot(u, wb_ref[:, :cols], preferred_element_type=F32)
        g = ((a * _sigmoid(a)) * b).astype(BF16)
        o_ref[...] += jnp.dot(g, wo_ref[:cols, :], preferred_element_type=F32)
        for src_ref, dst_ref in zip(cast_src, cast_dst):
            dst_ref[...] = src_ref[...].astype(dst_ref.dtype)

    tf = wa_ref.shape[1]
    if tail == tf:
        accumulate(tf)
    else:
        pl.when(j < n_f - 1)(lambda: accumulate(tf))
        pl.when(j == n_f - 1)(lambda: accumulate(tail))

    @pl.when(j == n_f - 1)
    def _():
        def body(i, carry):
            r = pl.multiple_of(i * NORM_ROWS, NORM_ROWS)
            rows = pl.ds(r, NORM_ROWS)
            out = h_ref[rows, :] + (0.5 * gate_ref[...]) * o_ref[rows, :]
            if final_norm:
                out = out * lax.rsqrt(jnp.mean(out * out, axis=-1, keepdims=True) + NORM_EPS) * fg_ref[...]
            o_ref[rows, :] = out
            return carry
        lax.fori_loop(0, o_ref.shape[0] // NORM_ROWS, body, 0, unroll=2)


def _ffn(h, gain, shift, scale, gate, w_a, w_b, w_out, seq, final_gain=None, cast_jobs=(), tm=1024, tf=512):
    T, D = h.shape
    d_ff = w_out.shape[0]
    n_f = pl.cdiv(d_ff, tf)
    tail = d_ff - (n_f - 1) * tf
    assert tail % LANES == 0
    per_b = seq // tm
    final_norm = final_gain is not None
    vec = pl.BlockSpec((1, D), lambda i, j: (0, 0))
    bvec = pl.BlockSpec((None, 1, D), lambda i, j: (i // per_b, 0, 0))
    in_specs = [
        pl.BlockSpec((tm, D), lambda i, j: (i, 0)),
        vec, bvec, bvec, bvec,
        pl.BlockSpec((D, tf), lambda i, j: (0, j)),
        pl.BlockSpec((D, tf), lambda i, j: (0, j)),
        pl.BlockSpec((tf, D), lambda i, j: (j, 0)),
    ]
    args = [h, gain, shift, scale, gate, w_a, w_b, w_out]
    if final_norm:
        in_specs.append(vec)
        args.append(final_gain)
    out_specs = [pl.BlockSpec((tm, D), lambda i, j: (i, 0))]
    out_shape = [jax.ShapeDtypeStruct((T, D), F32)]
    n_steps = (T // tm) * n_f
    for src, block_rows, width, col_block in cast_jobs:
        n_blocks = src.shape[0] // block_rows
        assert src.shape[0] % block_rows == 0 and n_blocks <= n_steps and block_rows % PACK_ROWS == 0

        def block_of(i, j, n_blocks=n_blocks):
            return jnp.minimum(i * n_f + j, n_blocks - 1)

        in_specs.append(pl.BlockSpec((block_rows, width), lambda i, j, f=block_of, cb=col_block: (f(i, j), cb)))
        args.append(src)
        out_specs.append(pl.BlockSpec((block_rows, width), lambda i, j, f=block_of: (f(i, j), 0)))
        out_shape.append(jax.ShapeDtypeStruct((src.shape[0], width), BF16))
    outs = pl.pallas_call(
        functools.partial(_ffn_kernel, n_f=n_f, tail=tail, final_norm=final_norm, n_cast=len(cast_jobs)),
        grid=(T // tm, n_f),
        in_specs=in_specs,
        out_specs=out_specs,
        out_shape=out_shape,
        scratch_shapes=[pltpu.VMEM((tm, D), BF16), pltpu.VMEM((tm, LANES), F32)],
        compiler_params=_params(("arbitrary", "arbitrary"), 60),
        name="ffn",
    )(*args)
    return outs[0], outs[1:]


def _rotate_block(z, cos, sin_lo, sin_hi):
    half = ROT_DIM // 2
    up = pltpu.roll(z, LANES - half, axis=1)
    down = pltpu.roll(z, half, axis=1)
    return z * cos + up * sin_lo + down * sin_hi


def _norm_proj_kernel(*refs, has_bias, rot_cols):
    refs = list(refs)
    h_ref, gain_ref, shift_ref, scale_ref, w_ref = refs[:5]
    pos = 5
    bias_ref = None
    if has_bias:
        bias_ref = refs[pos]
        pos += 1
    if rot_cols:
        cos_ref, slo_ref, shi_ref = refs[pos:pos + 3]
        pos += 3
    o_ref, u_scr, inv_scr = refs[pos:pos + 3]

    @pl.when(pl.program_id(1) == 0)
    def _():
        _ada_norm_into(u_scr, h_ref, gain_ref, shift_ref, scale_ref, inv_scr)

    z = jnp.dot(u_scr[...], w_ref[...], preferred_element_type=F32)
    if has_bias:
        z = z + bias_ref[...]
    if rot_cols:
        cos, slo, shi = cos_ref[...], slo_ref[...], shi_ref[...]
        tn = z.shape[1]
        blocks = []
        for cb in range(tn // LANES):
            zb = z[:, cb * LANES:(cb + 1) * LANES]
            if cb * LANES < rot_cols:
                zb = _rotate_block(zb, cos, slo, shi)
            blocks.append(zb)
        z = jnp.concatenate(blocks, axis=1) if len(blocks) > 1 else blocks[0]
    o_ref[...] = z.astype(o_ref.dtype)


def _norm_proj(h, gain, shift, scale, w, seq, out_dtype, bias=None, rope=None, rot_cols=0, tm=1024, tn=None):
    T, D = h.shape
    N = w.shape[1]
    tn = N if tn is None else tn
    per_b = seq // tm
    vec = pl.BlockSpec((1, D), lambda i, j: (0, 0))
    bvec = pl.BlockSpec((None, 1, D), lambda i, j: (i // per_b, 0, 0))
    in_specs = [pl.BlockSpec((tm, D), lambda i, j: (i, 0)), vec, bvec, bvec,
                pl.BlockSpec((D, tn), lambda i, j: (0, j))]
    args = [h, gain, shift, scale, w]
    if bias is not None:
        in_specs.append(pl.BlockSpec((1, tn), lambda i, j: (0, j)))
        args.append(bias)
    if rot_cols:
        assert tn == N and rot_cols % LANES == 0
        tab = pl.BlockSpec((tm, LANES), lambda i, j: (i % per_b, 0))
        in_specs += [tab, tab, tab]
        args += list(rope)
    return pl.pallas_call(
        functools.partial(_norm_proj_kernel, has_bias=bias is not None, rot_cols=rot_cols),
        grid=(T // tm, N // tn),
        in_specs=in_specs,
        out_specs=pl.BlockSpec((tm, tn), lambda i, j: (i, j)),
        out_shape=jax.ShapeDtypeStruct((T, N), out_dtype),
        scratch_shapes=[pltpu.VMEM((tm, D), BF16), pltpu.VMEM((tm, LANES), F32)],
        compiler_params=_params(("arbitrary", "arbitrary"), 48),
        name="norm_proj",
    )(*args)


def _proj_res_kernel(x_ref, w_ref, h_ref, gate_ref, o_ref):
    y = jnp.dot(x_ref[...], w_ref[...], preferred_element_type=F32)
    o_ref[...] = h_ref[...] + gate_ref[...] * y


def _proj_res(x, w, h, gate, seq, tm=512):
    T, K = x.shape
    N = w.shape[1]
    per_b = seq // tm
    return pl.pallas_call(
        _proj_res_kernel,
        grid=(T // tm,),
        in_specs=[
            pl.BlockSpec((tm, K), lambda i: (i, 0)),
            pl.BlockSpec((K, N), lambda i: (0, 0)),
            pl.BlockSpec((tm, N), lambda i: (i, 0)),
            pl.BlockSpec((None, 1, N), lambda i: (i // per_b, 0, 0)),
        ],
        out_specs=pl.BlockSpec((tm, N), lambda i: (i, 0)),
        out_shape=jax.ShapeDtypeStruct((T, N), F32),
        compiler_params=_params(("arbitrary",), 48),
        name="proj_res",
    )(x, w, h, gate)


def _split_bf16x3(x):
    hi = x.astype(BF16)
    r1 = x - hi.astype(F32)
    mid = r1.astype(BF16)
    lo = (r1 - mid.astype(F32)).astype(BF16)
    return hi, mid, lo


def _hgrn_scan_kernel(zq_ref, zf_ref, zi_ref, zg_ref, lbl_ref, hg_ref, o_ref,
                      st_ref, q_scr, b_scr, c_scr, v_scr, y_scr, bc_scr, cc_scr, mask_scr, spread_scr, tril_scr,
                      *, layer, n_chunks, n_heads):
    C = HGRN_CHUNK
    K = HGRN_HEAD
    levels = []
    m = C // 2
    while m >= SUBLANES:
        levels.append(m)
        m //= 2

    @pl.when(pl.program_id(2) == 0)
    def _():
        st_ref[...] = jnp.zeros_like(st_ref)
        row_c = lax.broadcasted_iota(jnp.int32, (C, C), 0)
        col_c = lax.broadcasted_iota(jnp.int32, (C, C), 1)
        tril = (row_c >= col_c).astype(BF16)
        tril_scr[...] = jnp.concatenate([tril, tril, tril], axis=1)
        xor_c = row_c ^ col_c
        mask_scr[0] = ((xor_c < SUBLANES) & (col_c <= row_c)).astype(F32)
        for li, m in enumerate(levels):
            if li:
                mask_scr[li] = (xor_c < 2 * m).astype(F32)
        src = lax.shift_right_logical(lax.broadcasted_iota(jnp.int32, (SUBLANES * K, C), 0), K.bit_length() - 1)
        dst = lax.broadcasted_iota(jnp.int32, (SUBLANES * K, C), 1) & (SUBLANES - 1)
        spread_scr[...] = (src == dst).astype(BF16)

    def prepare(hh, r0):
        lanes = slice(hh * K, (hh + 1) * K)
        rows = pl.ds(r0, C)
        logits = lbl_ref[:, lanes]
        e = jnp.exp(logits - jnp.max(logits, axis=0, keepdims=True))
        lb = jnp.sum(e[:layer + 1], axis=0, keepdims=True) / jnp.sum(e, axis=0, keepdims=True)

        zq = zq_ref[rows, lanes]
        q_scr[rows, lanes] = zq * _sigmoid(zq)
        forget = lb + (1.0 - lb) * _sigmoid(zf_ref[rows, lanes])
        v_scr[rows, lanes] = zi_ref[rows, lanes].astype(BF16)
        parts = jnp.concatenate(_split_bf16x3(jnp.log2(forget)), axis=0)
        b = jnp.dot(tril_scr[...], parts, preferred_element_type=F32)
        b_scr[rows, lanes] = b
        c_scr[rows, lanes] = b - jnp.log2(1.0 - forget)

    def mix(hh, r0):
        lanes = slice(hh * K, (hh + 1) * K)
        rows = pl.ds(r0, C)
        q = q_scr[rows, lanes]
        b = b_scr[rows, lanes]
        c = c_scr[rows, lanes]
        v16 = v_scr[rows, lanes]
        st = st_ref[hh]
        bc_scr[hh] = b
        cc_scr[hh] = c

        qd = (q * jnp.exp2(b)).astype(BF16)
        out = lax.dot_general(qd, st.astype(BF16), (((1,), (1,)), ((), ())), preferred_element_type=F32)
        b_last = bc_scr[hh, pl.ds(C - 1, 1), :]
        kd = jnp.exp2(b_last - c).astype(BF16)
        st_ref[hh] = st * jnp.exp2(b_last) + lax.dot_general(
            v16, kd, (((0,), (0,)), ((), ())), preferred_element_type=F32)

        w_rows = []
        for blk in range(C // SUBLANES):
            rb = blk * SUBLANES
            qb = q[rb:rb + SUBLANES]
            bb = b[rb:rb + SUBLANES]
            w_s = []
            for s in range(SUBLANES):
                rel = jnp.minimum(bb - cc_scr[hh, pl.ds(rb + s, 1), :], 0.0)
                w_s.append(qb * jnp.exp2(rel))
            w_rows.append(jnp.concatenate(w_s, axis=1))
        w_all = jnp.concatenate(w_rows, axis=0).astype(BF16)
        p = jnp.dot(w_all, spread_scr[...], preferred_element_type=F32) * mask_scr[0]

        n_groups = C // SUBLANES
        zero_group = jnp.zeros((SUBLANES, K), F32)
        for li, m in enumerate(levels):
            qt, kt = [], []
            for g in range(n_groups):
                rg = g * SUBLANES
                rows_g = slice(rg, rg + SUBLANES)
                anchor = (rg // (2 * m)) * 2 * m + m - 1
                ba = bc_scr[hh, pl.ds(anchor, 1), :]
                if rg & m:
                    qt.append(q[rows_g] * jnp.exp2(b[rows_g] - ba))
                    kt.append(zero_group)
                else:
                    qt.append(zero_group)
                    kt.append(jnp.exp2(ba - c[rows_g]))
            a_m = lax.dot_general(jnp.concatenate(qt, axis=0).astype(BF16),
                                  jnp.concatenate(kt, axis=0).astype(BF16),
                                  (((1,), (1,)), ((), ())), preferred_element_type=F32)
            p = p + (a_m if 2 * m == C else a_m * mask_scr[li])
        y_scr[rows, lanes] = out + jnp.dot(p.astype(BF16), v16, preferred_element_type=F32)

    def finish(hh, r0):
        lanes = slice(hh * K, (hh + 1) * K)
        rows = pl.ds(r0, C)
        out = y_scr[rows, lanes]
        y = out * lax.rsqrt(jnp.mean(out * out, axis=-1, keepdims=True) + NORM_EPS) * hg_ref[:, lanes]
        o_ref[rows, lanes] = (y * _sigmoid(zg_ref[rows, lanes])).astype(o_ref.dtype)

    def run(stage, ci):
        r0 = ci * C if isinstance(ci, int) else pl.multiple_of(ci * C, C)
        for hh in range(n_heads):
            stage(hh, r0)

    run(prepare, 0)
    run(mix, 0)
    run(prepare, 1)

    def trip(ci, carry):
        run(finish, ci - 1)
        run(mix, ci)
        run(prepare, jnp.minimum(ci + 1, n_chunks - 1))
        return carry
    lax.fori_loop(1, n_chunks, trip, 0)
    run(finish, n_chunks - 1)


def _hgrn_scan(z, lb_logits, head_gain, batch, seq, layer, ts=1024, hpb=4):
    assert HGRN_CHUNK == HGRN_HEAD
    T, D4 = z.shape
    D = D4 // 4
    n_s = seq // ts
    hw = hpb * HGRN_HEAD
    n_hb = D // hw

    def zspec(group):
        return pl.BlockSpec((ts, hw), lambda bi, hi, si: (bi * n_s + si, group * n_hb + hi))

    n_layers = lb_logits.shape[0]
    n_levels = (HGRN_CHUNK // SUBLANES).bit_length() - 1
    return pl.pallas_call(
        functools.partial(_hgrn_scan_kernel, layer=layer, n_chunks=ts // HGRN_CHUNK, n_heads=hpb),
        grid=(batch, n_hb, n_s),
        in_specs=[zspec(0), zspec(1), zspec(2), zspec(3),
                  pl.BlockSpec((n_layers, hw), lambda bi, hi, si: (0, hi)),
                  pl.BlockSpec((1, hw), lambda bi, hi, si: (0, hi))],
        out_specs=pl.BlockSpec((ts, hw), lambda bi, hi, si: (bi * n_s + si, hi)),
        out_shape=jax.ShapeDtypeStruct((T, D), BF16),
        scratch_shapes=[pltpu.VMEM((hpb, HGRN_HEAD, HGRN_HEAD), F32),
                        pltpu.VMEM((ts, hw), F32),
                        pltpu.VMEM((ts, hw), F32),
                        pltpu.VMEM((ts, hw), F32),
                        pltpu.VMEM((ts, hw), BF16),
                        pltpu.VMEM((ts, hw), F32),
                        pltpu.VMEM((hpb, HGRN_CHUNK, HGRN_HEAD), F32),
                        pltpu.VMEM((hpb, HGRN_CHUNK, HGRN_HEAD), F32),
                        pltpu.VMEM((n_levels, HGRN_CHUNK, HGRN_CHUNK), F32),
                        pltpu.VMEM((SUBLANES * HGRN_HEAD, HGRN_CHUNK), BF16),
                        pltpu.VMEM((HGRN_CHUNK, 3 * HGRN_CHUNK), BF16)],
        compiler_params=_params(("arbitrary", "arbitrary", "arbitrary"), 32),
        name="hgrn_scan",
    )(z, z, z, z, lb_logits, head_gain)


def _swa_kernel(sink_ref, q_ref, kc_ref, kp_ref, vc_ref, vp_ref, o_ref, bias_scr, *, n_blocks):
    W = WINDOW
    pair = pl.program_id(1)
    first_tile = pl.program_id(2) == 0
    lane_k = lax.broadcasted_iota(jnp.int32, (2 * W, LANES), 1)
    qi = lax.broadcasted_iota(jnp.int32, (W, 2 * W), 0)
    kj = lax.broadcasted_iota(jnp.int32, (W, 2 * W), 1)
    in_window = (kj > qi) & (kj <= qi + W)
    bias_scr[0] = jnp.where(in_window & (kj >= jnp.where(first_tile, W, 0)), 0.0, NEG_INF)
    bias_scr[1] = jnp.where(in_window, 0.0, NEG_INF)

    for n in range(n_blocks):
        rows = slice(n * W, (n + 1) * W)
        q2 = jnp.concatenate([q_ref[rows, j * LANES:(j + 1) * LANES] for j in range(GROUP)], axis=0)
        q2 = q2 * jnp.asarray(HEAD_DIM ** -0.5, q2.dtype)
        if n == 0:
            k_prev, v_prev = kp_ref[...], vp_ref[...]
        else:
            k_prev, v_prev = kc_ref[(n - 1) * W:n * W, :], vc_ref[(n - 1) * W:n * W, :]
        kk = jnp.concatenate([k_prev, kc_ref[rows, :]], axis=0)
        vv = jnp.concatenate([v_prev, vc_ref[rows, :]], axis=0)
        zero = jnp.zeros_like(kk)
        k_bd = jnp.concatenate([jnp.where(lane_k < HEAD_DIM, kk, zero),
                                jnp.where(lane_k >= HEAD_DIM, kk, zero)], axis=0)
        v_bd = jnp.concatenate([jnp.where(lane_k < HEAD_DIM, vv, zero),
                                jnp.where(lane_k >= HEAD_DIM, vv, zero)], axis=0)
        s = lax.dot_general(q2, k_bd, (((1,), (1,)), ((), ())), preferred_element_type=F32)
        bias = bias_scr[0 if n == 0 else 1]
        p_rows = []
        for j in range(GROUP):
            p_cols = []
            for gl in range(2):
                sub = s[j * W:(j + 1) * W, gl * 2 * W:(gl + 1) * 2 * W] + bias
                sink = sink_ref[(2 * pair + gl) * GROUP + j]
                mx = jnp.maximum(jnp.max(sub, axis=-1, keepdims=True), sink)
                p = jnp.exp(sub - mx)
                den = jnp.sum(p, axis=-1, keepdims=True) + jnp.exp(sink - mx)
                p_cols.append((p * (1.0 / den)).astype(BF16))
            p_rows.append(jnp.concatenate(p_cols, axis=1))
        probs = jnp.concatenate(p_rows, axis=0)
        out = jnp.dot(probs, v_bd, preferred_element_type=F32)
        for j in range(GROUP):
            o_ref[rows, j * LANES:(j + 1) * LANES] = out[j * W:(j + 1) * W].astype(o_ref.dtype)


def _swa(q, kv, sinks, batch, seq, tq=512):
    T, DQ = q.shape
    n_pairs = kv.shape[1] // (2 * LANES)
    qw = DQ // n_pairs
    n_t = seq // tq
    wpt = tq // WINDOW

    def cur(col0):
        return pl.BlockSpec((tq, LANES), lambda bi, pi, ti: (bi * n_t + ti, col0 + pi))

    def prev(col0):
        return pl.BlockSpec(
            (WINDOW, LANES),
            lambda bi, pi, ti: (jnp.maximum((bi * n_t + ti) * wpt - 1, 0), col0 + pi))

    return pl.pallas_call(
        functools.partial(_swa_kernel, n_blocks=wpt),
        grid=(batch, n_pairs, n_t),
        in_specs=[pl.BlockSpec(memory_space=pltpu.SMEM),
                  pl.BlockSpec((tq, qw), lambda bi, pi, ti: (bi * n_t + ti, pi)),
                  cur(0), prev(0), cur(n_pairs), prev(n_pairs)],
        out_specs=pl.BlockSpec((tq, qw), lambda bi, pi, ti: (bi * n_t + ti, pi)),
        out_shape=jax.ShapeDtypeStruct((T, DQ), BF16),
        scratch_shapes=[pltpu.VMEM((2, WINDOW, 2 * WINDOW), F32)],
        compiler_params=_params(("arbitrary", "arbitrary", "arbitrary"), 40),
        name="swa",
    )(sinks, q, kv, kv, kv, kv)


def _rope_lane_tables(seq):
    half = ROT_DIM // 2
    inv_freq = jnp.power(jnp.float32(ROPE_THETA), -jnp.arange(0, ROT_DIM, 2, dtype=F32) / ROT_DIM)
    ang = jnp.arange(seq, dtype=F32)[:, None] * inv_freq[None, :]
    sin, cos = jnp.sin(ang), jnp.cos(ang)
    ones = jnp.ones((seq, HEAD_DIM - ROT_DIM), F32)
    zeros_rest = jnp.zeros((seq, HEAD_DIM - ROT_DIM), F32)
    zeros_half = jnp.zeros((seq, half), F32)
    cos_h = jnp.concatenate([cos, cos, ones], axis=1)
    lo_h = jnp.concatenate([-sin, zeros_half, zeros_rest], axis=1)
    hi_h = jnp.concatenate([zeros_half, sin, zeros_rest], axis=1)
    reps = LANES // HEAD_DIM
    return tuple(jnp.tile(t, (1, reps)) for t in (cos_h, lo_h, hi_h))


def _pair_major_heads(n_q_heads):
    order = []
    for p in range(n_q_heads // (2 * GROUP)):
        for j in range(GROUP):
            for gl in range(2):
                order.append((2 * p + gl) * GROUP + j)
    return np.asarray(order, dtype=np.int32)


def kernel(x, c, norm_gain, w_ada, b_ada, w_ffn_in, w_ffn_out, w_hgrn_in, hgrn_lb_logits, hgrn_head_gain,
           w_hgrn_out, kv_gain, w_ada_kv, b_ada_kv, w_kv, b_kv, w_q, b_q, attn_sinks, w_attn_out, final_gain):
    B, S, D = x.shape
    T = B * S
    depth = w_ada.shape[0]
    n_a = w_hgrn_in.shape[0]
    d_ff = w_ffn_out.shape[2]
    n_q_heads = w_q.shape[2] // HEAD_DIM

    c_rows = jnp.pad(c, ((0, SUBLANES - B), (0, 0)))
    mod = _ada_mod(c_rows, w_ada, b_ada)[:, :B].reshape(depth, B, N_SUBLAYERS, 3, 1, D)
    mod_kv = _ada_mod(c_rows, w_ada_kv[None], b_ada_kv[None])[0, :B].reshape(B, 2, 1, D)

    heads = _pair_major_heads(n_q_heads)
    rope = _rope_lane_tables(S)

    def ffn_jobs(layer, k):
        w_in, w_o = w_ffn_in[layer, k], w_ffn_out[layer, k]
        return [(w_in, PACK_ROWS, d_ff, 0), (w_in, PACK_ROWS, d_ff, 1), (w_o, 2 * PACK_ROWS, D, 0)]

    def whole(w):
        return (w, PACK_ROWS, w.shape[1], 0)

    def mixer_weights(layer):
        if layer < n_a:
            return [w_hgrn_in[layer], w_hgrn_out[layer]]
        bl = layer - n_a
        wq = w_q[bl].reshape(D, n_q_heads, HEAD_DIM)[:, heads].reshape(D, -1)
        wo = w_attn_out[bl].reshape(n_q_heads, HEAD_DIM, D)[heads].reshape(-1, D)
        return [wq, wo]

    ffn_w = [w_ffn_in[0, 0, :, :d_ff].astype(BF16), w_ffn_in[0, 0, :, d_ff:].astype(BF16),
             w_ffn_out[0, 0].astype(BF16)]

    h = x.reshape(T, D)
    kv = None
    for layer in range(depth):
        def vecs(sub):
            return (norm_gain[layer, sub][None], mod[layer, :, sub, 0], mod[layer, :, sub, 1], mod[layer, :, sub, 2])

        gain, shift, scale, gate = vecs(0)
        jobs = [whole(w) for w in mixer_weights(layer)] + ffn_jobs(layer, 1)
        h, cast = _ffn(h, gain, shift, scale, gate, *ffn_w, S, cast_jobs=jobs)
        w_mix_in, w_mix_out, ffn_w = cast[0], cast[1], cast[2:]

        gain, shift, scale, gate = vecs(1)
        if layer < n_a:
            z = _norm_proj(h, gain, shift, scale, w_mix_in, S, F32, tn=1024)
            o = _hgrn_scan(z, hgrn_lb_logits, hgrn_head_gain[layer][None], B, S, layer)
        else:
            bl = layer - n_a
            bq = b_q[bl].reshape(n_q_heads, HEAD_DIM)[heads].reshape(1, -1)
            q = _norm_proj(h, gain, shift, scale, w_mix_in, S, BF16, bias=bq, rope=rope,
                           rot_cols=w_mix_in.shape[1], tm=512)
            o = _swa(q, kv, attn_sinks[bl], B, S)
        h = _proj_res(o, w_mix_out, h, gate, S)

        gain, shift, scale, gate = vecs(2)
        last = layer == depth - 1
        jobs = ([whole(w_kv)] if layer == n_a - 1 else []) + ([] if last else ffn_jobs(layer + 1, 0))
        h, cast = _ffn(h, gain, shift, scale, gate, *ffn_w, S, cast_jobs=jobs,
                       final_gain=final_gain[None] if last else None)

        if layer == n_a - 1:
            w_kv16, cast = cast[0], cast[1:]
            kv = _norm_proj(h, kv_gain[None], mod_kv[:, 0], mod_kv[:, 1], w_kv16, S, BF16,
                            bias=b_kv[None], rope=rope, rot_cols=w_kv.shape[1] // 2, tm=512)
        ffn_w = cast

    return h.reshape(B, S, D)
```

```python
import functools

import numpy as np
import jax
import jax.numpy as jnp
from jax import lax
from jax.experimental import pallas as pl
from jax.experimental.pallas import tpu as pltpu

F32 = jnp.float32
BF16 = jnp.bfloat16

NORM_EPS = 1e-6
NEG_INF = -1e30
ROPE_THETA = 500000.0

LANES = 128
SUBLANES = 8
MIB = 1 << 20

HGRN_HEAD = 128
HGRN_CHUNK = 128
HEAD_DIM = 64
ROT_DIM = HEAD_DIM // 4
GROUP = 8
WINDOW = 128
N_SUBLAYERS = 3


def _params(semantics, vmem_mib):
    return pltpu.CompilerParams(dimension_semantics=semantics, vmem_limit_bytes=vmem_mib * MIB)


def _sigmoid(x):
    return 1.0 / (1.0 + jnp.exp(-x))


def _ada_norm(h, gain, shift, scale):
    y = h * lax.rsqrt(jnp.mean(h * h, axis=-1, keepdims=True) + NORM_EPS)
    return (y * gain) * (1.0 + scale) + shift


NORM_ROWS = 32
PACK_ROWS = 16


def _ada_norm_into(u_ref, h_ref, gain_ref, shift_ref, scale_ref, inv_scr, zero_ref=None):
    n_rows, d = h_ref.shape
    n_cb = d // LANES

    def pass1(i, carry):
        r = pl.multiple_of(i * SUBLANES, SUBLANES)
        acc = None
        for cb in range(n_cb):
            xb = h_ref[pl.ds(r, SUBLANES), cb * LANES:(cb + 1) * LANES]
            acc = xb * xb if acc is None else acc + xb * xb
        inv_scr[pl.ds(r, SUBLANES), :] = acc
        return carry
    lax.fori_loop(0, n_rows // SUBLANES, pass1, 0, unroll=4)
    ones = jnp.ones((LANES, LANES), BF16)
    ssq = None
    for part in _split_bf16x3(inv_scr[...]):
        s = jnp.dot(part, ones, preferred_element_type=F32)
        ssq = s if ssq is None else ssq + s
    inv_scr[...] = lax.rsqrt(ssq * (1.0 / d) + NORM_EPS)

    mult = gain_ref[...] * (1.0 + scale_ref[...])
    shift = shift_ref[...]

    def pass2(i, carry):
        r = pl.multiple_of(i * PACK_ROWS, PACK_ROWS)
        rows = pl.ds(r, PACK_ROWS)
        inv = inv_scr[rows, :]
        for cb in range(n_cb):
            cols = slice(cb * LANES, (cb + 1) * LANES)
            u_ref[rows, cols] = ((h_ref[rows, cols] * inv) * mult[:, cols] + shift[:, cols]).astype(u_ref.dtype)
            if zero_ref is not None:
                zero_ref[rows, cols] = jnp.zeros((PACK_ROWS, LANES), zero_ref.dtype)
        return carry
    lax.fori_loop(0, n_rows // PACK_ROWS, pass2, 0, unroll=2)


def _mod_kernel(c_ref, w_ref, b_ref, o_ref):
    c = c_ref[...]
    cs = (c * _sigmoid(c)).astype(BF16)
    o_ref[...] = jnp.dot(cs, w_ref[...].astype(BF16), preferred_element_type=F32) + b_ref[...]


def _ada_mod(c_rows, w, b):
    L, D, N = w.shape
    tn = 1024
    return pl.pallas_call(
        _mod_kernel,
        grid=(L, N // tn),
        in_specs=[
            pl.BlockSpec((SUBLANES, D), lambda l, j: (0, 0)),
            pl.BlockSpec((None, D, tn), lambda l, j: (l, 0, j)),
            pl.BlockSpec((None, 1, tn), lambda l, j: (l, 0, j)),
        ],
        out_specs=pl.BlockSpec((None, SUBLANES, tn), lambda l, j: (l, 0, j)),
        out_shape=jax.ShapeDtypeStruct((L, SUBLANES, N), F32),
        compiler_params=_params(("arbitrary", "arbitrary"), 40),
        name="ada_mod",
    )(c_rows, w, b.reshape(L, 1, N))


def _cast_specs(jobs, n_steps, step_of):
    in_specs, out_specs, out_shape = [], [], []
    for src, lead, block_rows, width, col_block in jobs:
        n_rows = src.shape[len(lead)]
        n_blocks = n_rows // block_rows
        assert n_rows % block_rows == 0 and n_blocks <= n_steps and block_rows % PACK_ROWS == 0

        def block_of(*idx, n_blocks=n_blocks):
            return jnp.minimum(step_of(*idx), n_blocks - 1)

        in_specs.append(pl.BlockSpec((None,) * len(lead) + (block_rows, width),
                                     lambda *idx, f=block_of, lead=lead, cb=col_block: lead + (f(*idx), cb)))
        out_specs.append(pl.BlockSpec((block_rows, width), lambda *idx, f=block_of: (f(*idx), 0)))
        out_shape.append(jax.ShapeDtypeStruct((n_rows, width), BF16))
    return in_specs, out_specs, out_shape


def _cast_kernel(*refs):
    n = len(refs) // 2
    for src_ref, dst_ref in zip(refs[:n], refs[n:]):
        dst_ref[...] = src_ref[...].astype(dst_ref.dtype)


def _cast_weights(jobs, n_steps):
    in_specs, out_specs, out_shape = _cast_specs(jobs, n_steps, lambda s: s)
    return pl.pallas_call(
        _cast_kernel,
        grid=(n_steps,),
        in_specs=in_specs,
        out_specs=out_specs,
        out_shape=out_shape,
        compiler_params=_params(("arbitrary",), 32),
        name="cast_weights",
    )(*[job[0] for job in jobs])


def _ffn_kernel(*refs, n_f, tail, final_norm, n_cast):
    refs = list(refs)
    h_ref, gain_ref, shift_ref, scale_ref, gate_ref, wa_ref, wb_ref, wo_ref = refs[:8]
    pos = 8
    fg_ref = None
    if final_norm:
        fg_ref = refs[pos]
        pos += 1
    cast_src = refs[pos:pos + n_cast]
    o_ref = refs[pos + n_cast]
    cast_dst = refs[pos + n_cast + 1:pos + 2 * n_cast + 1]
    u_scr, inv_scr = refs[pos + 2 * n_cast + 1:]
    j = pl.program_id(1)

    @pl.when(j == 0)
    def _():
        _ada_norm_into(u_scr, h_ref, gain_ref, shift_ref, scale_ref, inv_scr, zero_ref=o_ref)

    def accumulate(cols):
        u = u_scr[...]
        a = jnp.dot(u, wa_ref[:, :cols], preferred_element_type=F32)
        b = jnp.dot(u, wb_ref[:, :cols], preferred_element_type=F32)
        g = ((a * _sigmoid(a)) * b).astype(BF16)
        o_ref[...] += jnp.dot(g, wo_ref[:cols, :], preferred_element_type=F32)
        for src_ref, dst_ref in zip(cast_src, cast_dst):
            dst_ref[...] = src_ref[...].astype(dst_ref.dtype)

    tf = wa_ref.shape[1]
    if tail == tf:
        accumulate(tf)
    else:
        pl.when(j < n_f - 1)(lambda: accumulate(tf))
        pl.when(j == n_f - 1)(lambda: accumulate(tail))

    @pl.when(j == n_f - 1)
    def _():
        def body(i, carry):
            r = pl.multiple_of(i * NORM_ROWS, NORM_ROWS)
            rows = pl.ds(r, NORM_ROWS)
            out = h_ref[rows, :] + (0.5 * gate_ref[...]) * o_ref[rows, :]
            if final_norm:
                out = out * lax.rsqrt(jnp.mean(out * out, axis=-1, keepdims=True) + NORM_EPS) * fg_ref[...]
            o_ref[rows, :] = out
            return carry
        lax.fori_loop(0, o_ref.shape[0] // NORM_ROWS, body, 0, unroll=2)


def _ffn(h, gain, shift, scale, gate, w_a, w_b, w_out, seq, final_gain=None, cast_jobs=(), tm=1024, tf=512):
    T, D = h.shape
    d_ff = w_out.shape[0]
    n_f = pl.cdiv(d_ff, tf)
    tail = d_ff - (n_f - 1) * tf
    assert tail % LANES == 0
    per_b = seq // tm
    final_norm = final_gain is not None
    vec = pl.BlockSpec((1, D), lambda i, j: (0, 0))
    bvec = pl.BlockSpec((None, 1, D), lambda i, j: (i // per_b, 0, 0))
    in_specs = [
        pl.BlockSpec((tm, D), lambda i, j: (i, 0)),
        vec, bvec, bvec, bvec,
        pl.BlockSpec((D, tf), lambda i, j: (0, j)),
        pl.BlockSpec((D, tf), lambda i, j: (0, j)),
        pl.BlockSpec((tf, D), lambda i, j: (j, 0)),
    ]
    args = [h, gain, shift, scale, gate, w_a, w_b, w_out]
    if final_norm:
        in_specs.append(vec)
        args.append(final_gain)
    out_specs = [pl.BlockSpec((tm, D), lambda i, j: (i, 0))]
    out_shape = [jax.ShapeDtypeStruct((T, D), F32)]
    cast_in, cast_out, cast_shape = _cast_specs(cast_jobs, (T // tm) * n_f, lambda i, j: i * n_f + j)
    in_specs += cast_in
    args += [job[0] for job in cast_jobs]
    out_specs += cast_out
    out_shape += cast_shape
    outs = pl.pallas_call(
        functools.partial(_ffn_kernel, n_f=n_f, tail=tail, final_norm=final_norm, n_cast=len(cast_jobs)),
        grid=(T // tm, n_f),
        in_specs=in_specs,
        out_specs=out_specs,
        out_shape=out_shape,
        scratch_shapes=[pltpu.VMEM((tm, D), BF16), pltpu.VMEM((tm, LANES), F32)],
        compiler_params=_params(("arbitrary", "arbitrary"), 60),
        name="ffn",
    )(*args)
    return outs[0], outs[1:]


def _rotate_block(z, cos, sin_lo, sin_hi):
    half = ROT_DIM // 2
    up = pltpu.roll(z, LANES - half, axis=1)
    down = pltpu.roll(z, half, axis=1)
    return z * cos + up * sin_lo + down * sin_hi


def _norm_proj_kernel(*refs, has_bias, rot_cols):
    refs = list(refs)
    h_ref, gain_ref, shift_ref, scale_ref, w_ref = refs[:5]
    pos = 5
    bias_ref = None
    if has_bias:
        bias_ref = refs[pos]
        pos += 1
    if rot_cols:
        cos_ref, slo_ref, shi_ref = refs[pos:pos + 3]
        pos += 3
    o_ref, u_scr, inv_scr = refs[pos:pos + 3]

    @pl.when(pl.program_id(1) == 0)
    def _():
        _ada_norm_into(u_scr, h_ref, gain_ref, shift_ref, scale_ref, inv_scr)

    z = jnp.dot(u_scr[...], w_ref[...], preferred_element_type=F32)
    if has_bias:
        z = z + bias_ref[...]
    if rot_cols:
        cos, slo, shi = cos_ref[...], slo_ref[...], shi_ref[...]
        tn = z.shape[1]
        blocks = []
        for cb in range(tn // LANES):
            zb = z[:, cb * LANES:(cb + 1) * LANES]
            if cb * LANES < rot_cols:
                zb = _rotate_block(zb, cos, slo, shi)
            blocks.append(zb)
        z = jnp.concatenate(blocks, axis=1) if len(blocks) > 1 else blocks[0]
    o_ref[...] = z.astype(o_ref.dtype)


def _norm_proj(h, gain, shift, scale, w, seq, out_dtype, bias=None, rope=None, rot_cols=0, tm=1024, tn=None):
    T, D = h.shape
    N = w.shape[1]
    tn = N if tn is None else tn
    per_b = seq // tm
    vec = pl.BlockSpec((1, D), lambda i, j: (0, 0))
    bvec = pl.BlockSpec((None, 1, D), lambda i, j: (i // per_b, 0, 0))
    in_specs = [pl.BlockSpec((tm, D), lambda i, j: (i, 0)), vec, bvec, bvec,
                pl.BlockSpec((D, tn), lambda i, j: (0, j))]
    args = [h, gain, shift, scale, w]
    if bias is not None:
        in_specs.append(pl.BlockSpec((1, tn), lambda i, j: (0, j)))
        args.append(bias)
    if rot_cols:
        assert tn == N and rot_cols % LANES == 0
        tab = pl.BlockSpec((tm, LANES), lambda i, j: (i % per_b, 0))
        in_specs += [tab, tab, tab]
        args += list(rope)
    return pl.pallas_call(
        functools.partial(_norm_proj_kernel, has_bias=bias is not None, rot_cols=rot_cols),
        grid=(T // tm, N // tn),
        in_specs=in_specs,
        out_specs=pl.BlockSpec((tm, tn), lambda i, j: (i, j)),
        out_shape=jax.ShapeDtypeStruct((T, N), out_dtype),
        scratch_shapes=[pltpu.VMEM((tm, D), BF16), pltpu.VMEM((tm, LANES), F32)],
        compiler_params=_params(("arbitrary", "arbitrary"), 48),
        name="norm_proj",
    )(*args)


def _proj_res_kernel(x_ref, w_ref, h_ref, gate_ref, o_ref):
    y = jnp.dot(x_ref[...], w_ref[...], preferred_element_type=F32)
    o_ref[...] = h_ref[...] + gate_ref[...] * y


def _proj_res(x, w, h, gate, seq, tm=512):
    T, K = x.shape
    N = w.shape[1]
    per_b = seq // tm
    return pl.pallas_call(
        _proj_res_kernel,
        grid=(T // tm,),
        in_specs=[
            pl.BlockSpec((tm, K), lambda i: (i, 0)),
            pl.BlockSpec((K, N), lambda i: (0, 0)),
            pl.BlockSpec((tm, N), lambda i: (i, 0)),
            pl.BlockSpec((None, 1, N), lambda i: (i // per_b, 0, 0)),
        ],
        out_specs=pl.BlockSpec((tm, N), lambda i: (i, 0)),
        out_shape=jax.ShapeDtypeStruct((T, N), F32),
        compiler_params=_params(("arbitrary",), 48),
        name="proj_res",
    )(x, w, h, gate)


def _split_bf16x3(x):
    hi = x.astype(BF16)
    r1 = x - hi.astype(F32)
    mid = r1.astype(BF16)
    lo = (r1 - mid.astype(F32)).astype(BF16)
    return hi, mid, lo


def _hgrn_scan_kernel(zq_ref, zf_ref, zi_ref, zg_ref, lbl_ref, hg_ref, o_ref,
                      st_ref, q_scr, b_scr, c_scr, v_scr, y_scr, bc_scr, cc_scr, mask_scr, spread_scr, tril_scr,
                      *, layer, n_chunks, n_heads):
    C = HGRN_CHUNK
    K = HGRN_HEAD
    levels = []
    m = C // 2
    while m >= SUBLANES:
        levels.append(m)
        m //= 2

    @pl.when(pl.program_id(2) == 0)
    def _():
        st_ref[...] = jnp.zeros_like(st_ref)
        row_c = lax.broadcasted_iota(jnp.int32, (C, C), 0)
        col_c = lax.broadcasted_iota(jnp.int32, (C, C), 1)
        tril = (row_c >= col_c).astype(BF16)
        tril_scr[...] = jnp.concatenate([tril, tril, tril], axis=1)
        xor_c = row_c ^ col_c
        mask_scr[0] = ((xor_c < SUBLANES) & (col_c <= row_c)).astype(F32)
        for li, m in enumerate(levels):
            if li:
                mask_scr[li] = (xor_c < 2 * m).astype(F32)
        src = lax.shift_right_logical(lax.broadcasted_iota(jnp.int32, (SUBLANES * K, C), 0), K.bit_length() - 1)
        dst = lax.broadcasted_iota(jnp.int32, (SUBLANES * K, C), 1) & (SUBLANES - 1)
        spread_scr[...] = (src == dst).astype(BF16)

    def prepare(hh, r0):
        lanes = slice(hh * K, (hh + 1) * K)
        rows = pl.ds(r0, C)
        logits = lbl_ref[:, lanes]
        e = jnp.exp(logits - jnp.max(logits, axis=0, keepdims=True))
        lb = jnp.sum(e[:layer + 1], axis=0, keepdims=True) / jnp.sum(e, axis=0, keepdims=True)

        zq = zq_ref[rows, lanes]
        q_scr[rows, lanes] = zq * _sigmoid(zq)
        forget = lb + (1.0 - lb) * _sigmoid(zf_ref[rows, lanes])
        v_scr[rows, lanes] = zi_ref[rows, lanes].astype(BF16)
        parts = jnp.concatenate(_split_bf16x3(jnp.log2(forget)), axis=0)
        b = jnp.dot(tril_scr[...], parts, preferred_element_type=F32)
        b_scr[rows, lanes] = b
        c_scr[rows, lanes] = b - jnp.log2(1.0 - forget)

    def mix(hh, r0):
        lanes = slice(hh * K, (hh + 1) * K)
        rows = pl.ds(r0, C)
        q = q_scr[rows, lanes]
        b = b_scr[rows, lanes]
        c = c_scr[rows, lanes]
        v16 = v_scr[rows, lanes]
        st = st_ref[hh]
        bc_scr[hh] = b
        cc_scr[hh] = c

        qd = (q * jnp.exp2(b)).astype(BF16)
        out = lax.dot_general(qd, st.astype(BF16), (((1,), (1,)), ((), ())), preferred_element_type=F32)
        b_last = bc_scr[hh, pl.ds(C - 1, 1), :]
        kd = jnp.exp2(b_last - c).astype(BF16)
        st_ref[hh] = st * jnp.exp2(b_last) + lax.dot_general(
            v16, kd, (((0,), (0,)), ((), ())), preferred_element_type=F32)

        w_rows = []
        for blk in range(C // SUBLANES):
            rb = blk * SUBLANES
            qb = q[rb:rb + SUBLANES]
            bb = b[rb:rb + SUBLANES]
            w_s = []
            for s in range(SUBLANES):
                rel = jnp.minimum(bb - cc_scr[hh, pl.ds(rb + s, 1), :], 0.0)
                w_s.append(qb * jnp.exp2(rel))
            w_rows.append(jnp.concatenate(w_s, axis=1))
        w_all = jnp.concatenate(w_rows, axis=0).astype(BF16)
        p = jnp.dot(w_all, spread_scr[...], preferred_element_type=F32) * mask_scr[0]

        n_groups = C // SUBLANES
        zero_group = jnp.zeros((SUBLANES, K), F32)
        for li, m in enumerate(levels):
            qt, kt = [], []
            for g in range(n_groups):
                rg = g * SUBLANES
                rows_g = slice(rg, rg + SUBLANES)
                anchor = (rg // (2 * m)) * 2 * m + m - 1
                ba = bc_scr[hh, pl.ds(anchor, 1), :]
                if rg & m:
                    qt.append(q[rows_g] * jnp.exp2(b[rows_g] - ba))
                    kt.append(zero_group)
                else:
                    qt.append(zero_group)
                    kt.append(jnp.exp2(ba - c[rows_g]))
            a_m = lax.dot_general(jnp.concatenate(qt, axis=0).astype(BF16),
                                  jnp.concatenate(kt, axis=0).astype(BF16),
                                  (((1,), (1,)), ((), ())), preferred_element_type=F32)
            p = p + (a_m if 2 * m == C else a_m * mask_scr[li])
        y_scr[rows, lanes] = out + jnp.dot(p.astype(BF16), v16, preferred_element_type=F32)

    def finish(hh, r0):
        lanes = slice(hh * K, (hh + 1) * K)
        rows = pl.ds(r0, C)
        out = y_scr[rows, lanes]
        y = out * lax.rsqrt(jnp.mean(out * out, axis=-1, keepdims=True) + NORM_EPS) * hg_ref[:, lanes]
        o_ref[rows, lanes] = (y * _sigmoid(zg_ref[rows, lanes])).astype(o_ref.dtype)

    def run(stage, ci):
        r0 = ci * C if isinstance(ci, int) else pl.multiple_of(ci * C, C)
        for hh in range(n_heads):
            stage(hh, r0)

    run(prepare, 0)
    run(mix, 0)
    run(prepare, 1)

    def trip(ci, carry):
        run(finish, ci - 1)
        run(mix, ci)
        run(prepare, jnp.minimum(ci + 1, n_chunks - 1))
        return carry
    lax.fori_loop(1, n_chunks, trip, 0)
    run(finish, n_chunks - 1)


def _hgrn_scan(z, lb_logits, head_gain, batch, seq, layer, ts=1024, hpb=4):
    assert HGRN_CHUNK == HGRN_HEAD
    T, D4 = z.shape
    D = D4 // 4
    n_s = seq // ts
    hw = hpb * HGRN_HEAD
    n_hb = D // hw

    def zspec(group):
        return pl.BlockSpec((ts, hw), lambda bi, hi, si: (bi * n_s + si, group * n_hb + hi))

    n_layers = lb_logits.shape[0]
    n_levels = (HGRN_CHUNK // SUBLANES).bit_length() - 1
    return pl.pallas_call(
        functools.partial(_hgrn_scan_kernel, layer=layer, n_chunks=ts // HGRN_CHUNK, n_heads=hpb),
        grid=(batch, n_hb, n_s),
        in_specs=[zspec(0), zspec(1), zspec(2), zspec(3),
                  pl.BlockSpec((n_layers, hw), lambda bi, hi, si: (0, hi)),
                  pl.BlockSpec((1, hw), lambda bi, hi, si: (0, hi))],
        out_specs=pl.BlockSpec((ts, hw), lambda bi, hi, si: (bi * n_s + si, hi)),
        out_shape=jax.ShapeDtypeStruct((T, D), BF16),
        scratch_shapes=[pltpu.VMEM((hpb, HGRN_HEAD, HGRN_HEAD), F32),
                        pltpu.VMEM((ts, hw), F32),
                        pltpu.VMEM((ts, hw), F32),
                        pltpu.VMEM((ts, hw), F32),
                        pltpu.VMEM((ts, hw), BF16),
                        pltpu.VMEM((ts, hw), F32),
                        pltpu.VMEM((hpb, HGRN_CHUNK, HGRN_HEAD), F32),
                        pltpu.VMEM((hpb, HGRN_CHUNK, HGRN_HEAD), F32),
                        pltpu.VMEM((n_levels, HGRN_CHUNK, HGRN_CHUNK), F32),
                        pltpu.VMEM((SUBLANES * HGRN_HEAD, HGRN_CHUNK), BF16),
                        pltpu.VMEM((HGRN_CHUNK, 3 * HGRN_CHUNK), BF16)],
        compiler_params=_params(("arbitrary", "arbitrary", "arbitrary"), 32),
        name="hgrn_scan",
    )(z, z, z, z, lb_logits, head_gain)


def _swa_kernel(sink_ref, q_ref, kc_ref, kp_ref, vc_ref, vp_ref, o_ref, bias_scr, *, n_blocks):
    W = WINDOW
    pair = pl.program_id(1)
    first_tile = pl.program_id(2) == 0
    lane_k = lax.broadcasted_iota(jnp.int32, (2 * W, LANES), 1)
    qi = lax.broadcasted_iota(jnp.int32, (W, 2 * W), 0)
    kj = lax.broadcasted_iota(jnp.int32, (W, 2 * W), 1)
    in_window = (kj > qi) & (kj <= qi + W)
    bias_scr[0] = jnp.where(in_window & (kj >= jnp.where(first_tile, W, 0)), 0.0, NEG_INF)
    bias_scr[1] = jnp.where(in_window, 0.0, NEG_INF)

    for n in range(n_blocks):
        rows = slice(n * W, (n + 1) * W)
        q2 = jnp.concatenate([q_ref[rows, j * LANES:(j + 1) * LANES] for j in range(GROUP)], axis=0)
        q2 = q2 * jnp.asarray(HEAD_DIM ** -0.5, q2.dtype)
        if n == 0:
            k_prev, v_prev = kp_ref[...], vp_ref[...]
        else:
            k_prev, v_prev = kc_ref[(n - 1) * W:n * W, :], vc_ref[(n - 1) * W:n * W, :]
        kk = jnp.concatenate([k_prev, kc_ref[rows, :]], axis=0)
        vv = jnp.concatenate([v_prev, vc_ref[rows, :]], axis=0)
        zero = jnp.zeros_like(kk)
        k_bd = jnp.concatenate([jnp.where(lane_k < HEAD_DIM, kk, zero),
                                jnp.where(lane_k >= HEAD_DIM, kk, zero)], axis=0)
        v_bd = jnp.concatenate([jnp.where(lane_k < HEAD_DIM, vv, zero),
                                jnp.where(lane_k >= HEAD_DIM, vv, zero)], axis=0)
        s = lax.dot_general(q2, k_bd, (((1,), (1,)), ((), ())), preferred_element_type=F32)
        bias = bias_scr[0 if n == 0 else 1]
        p_rows = []
        for j in range(GROUP):
            p_cols = []
            for gl in range(2):
                sub = s[j * W:(j + 1) * W, gl * 2 * W:(gl + 1) * 2 * W] + bias
                sink = sink_ref[(2 * pair + gl) * GROUP + j]
                mx = jnp.maximum(jnp.max(sub, axis=-1, keepdims=True), sink)
                p = jnp.exp(sub - mx)
                den = jnp.sum(p, axis=-1, keepdims=True) + jnp.exp(sink - mx)
                p_cols.append((p * (1.0 / den)).astype(BF16))
            p_rows.append(jnp.concatenate(p_cols, axis=1))
        probs = jnp.concatenate(p_rows, axis=0)
        out = jnp.dot(probs, v_bd, preferred_element_type=F32)
        for j in range(GROUP):
            o_ref[rows, j * LANES:(j + 1) * LANES] = out[j * W:(j + 1) * W].astype(o_ref.dtype)


def _swa(q, kv, sinks, batch, seq, tq=512):
    T, DQ = q.shape
    n_pairs = kv.shape[1] // (2 * LANES)
    qw = DQ // n_pairs
    n_t = seq // tq
    wpt = tq // WINDOW

    def cur(col0):
        return pl.BlockSpec((tq, LANES), lambda bi, pi, ti: (bi * n_t + ti, col0 + pi))

    def prev(col0):
        return pl.BlockSpec(
            (WINDOW, LANES),
            lambda bi, pi, ti: (jnp.maximum((bi * n_t + ti) * wpt - 1, 0), col0 + pi))

    return pl.pallas_call(
        functools.partial(_swa_kernel, n_blocks=wpt),
        grid=(batch, n_pairs, n_t),
        in_specs=[pl.BlockSpec(memory_space=pltpu.SMEM),
                  pl.BlockSpec((tq, qw), lambda bi, pi, ti: (bi * n_t + ti, pi)),
                  cur(0), prev(0), cur(n_pairs), prev(n_pairs)],
        out_specs=pl.BlockSpec((tq, qw), lambda bi, pi, ti: (bi * n_t + ti, pi)),
        out_shape=jax.ShapeDtypeStruct((T, DQ), BF16),
        scratch_shapes=[pltpu.VMEM((2, WINDOW, 2 * WINDOW), F32)],
        compiler_params=_params(("arbitrary", "arbitrary", "arbitrary"), 40),
        name="swa",
    )(sinks, q, kv, kv, kv, kv)


def _rope_lane_tables(seq):
    half = ROT_DIM // 2
    inv_freq = jnp.power(jnp.float32(ROPE_THETA), -jnp.arange(0, ROT_DIM, 2, dtype=F32) / ROT_DIM)
    ang = jnp.arange(seq, dtype=F32)[:, None] * inv_freq[None, :]
    sin, cos = jnp.sin(ang), jnp.cos(ang)
    ones = jnp.ones((seq, HEAD_DIM - ROT_DIM), F32)
    zeros_rest = jnp.zeros((seq, HEAD_DIM - ROT_DIM), F32)
    zeros_half = jnp.zeros((seq, half), F32)
    cos_h = jnp.concatenate([cos, cos, ones], axis=1)
    lo_h = jnp.concatenate([-sin, zeros_half, zeros_rest], axis=1)
    hi_h = jnp.concatenate([zeros_half, sin, zeros_rest], axis=1)
    reps = LANES // HEAD_DIM
    return tuple(jnp.tile(t, (1, reps)) for t in (cos_h, lo_h, hi_h))


def _pair_major_heads(n_q_heads):
    order = []
    for p in range(n_q_heads // (2 * GROUP)):
        for j in range(GROUP):
            for gl in range(2):
                order.append((2 * p + gl) * GROUP + j)
    return np.asarray(order, dtype=np.int32)


def kernel(x, c, norm_gain, w_ada, b_ada, w_ffn_in, w_ffn_out, w_hgrn_in, hgrn_lb_logits, hgrn_head_gain,
           w_hgrn_out, kv_gain, w_ada_kv, b_ada_kv, w_kv, b_kv, w_q, b_q, attn_sinks, w_attn_out, final_gain):
    B, S, D = x.shape
    T = B * S
    depth = w_ada.shape[0]
    n_a = w_hgrn_in.shape[0]
    d_ff = w_ffn_out.shape[2]
    n_q_heads = w_q.shape[2] // HEAD_DIM

    c_rows = jnp.pad(c, ((0, SUBLANES - B), (0, 0)))
    mod = _ada_mod(c_rows, w_ada, b_ada)[:, :B].reshape(depth, B, N_SUBLAYERS, 3, 1, D)
    mod_kv = _ada_mod(c_rows, w_ada_kv[None], b_ada_kv[None])[0, :B].reshape(B, 2, 1, D)

    heads = _pair_major_heads(n_q_heads)
    rope = _rope_lane_tables(S)

    def ffn_jobs(layer, k, rows=PACK_ROWS):
        return [(w_ffn_in, (layer, k), rows, d_ff, 0), (w_ffn_in, (layer, k), rows, d_ff, 1),
                (w_ffn_out, (layer, k), 2 * rows, D, 0)]

    def whole(w, lead=()):
        return (w, lead, PACK_ROWS, w.shape[-1], 0)

    def mixer_jobs(layer):
        if layer < n_a:
            return [whole(w_hgrn_in, (layer,)), whole(w_hgrn_out, (layer,))]
        bl = layer - n_a
        wq = w_q[bl].reshape(D, n_q_heads, HEAD_DIM)[:, heads].reshape(D, -1)
        wo = w_attn_out[bl].reshape(n_q_heads, HEAD_DIM, D)[heads].reshape(-1, D)
        return [whole(wq), whole(wo)]

    first = ffn_jobs(0, 0, rows=4 * PACK_ROWS)
    ffn_w = _cast_weights(first, max(job[0].shape[-2] // job[2] for job in first))

    h = x.reshape(T, D)
    kv = None
    for layer in range(depth):
        def vecs(sub):
            return (norm_gain[layer, sub][None], mod[layer, :, sub, 0], mod[layer, :, sub, 1], mod[layer, :, sub, 2])

        gain, shift, scale, gate = vecs(0)
        jobs = mixer_jobs(layer) + ffn_jobs(layer, 1)
        h, cast = _ffn(h, gain, shift, scale, gate, *ffn_w, S, cast_jobs=jobs)
        w_mix_in, w_mix_out, ffn_w = cast[0], cast[1], cast[2:]

        gain, shift, scale, gate = vecs(1)
        if layer < n_a:
            z = _norm_proj(h, gain, shift, scale, w_mix_in, S, F32, tn=1024)
            o = _hgrn_scan(z, hgrn_lb_logits, hgrn_head_gain[layer][None], B, S, layer)
        else:
            bl = layer - n_a
            bq = b_q[bl].reshape(n_q_heads, HEAD_DIM)[heads].reshape(1, -1)
            q = _norm_proj(h, gain, shift, scale, w_mix_in, S, BF16, bias=bq, rope=rope,
                           rot_cols=w_mix_in.shape[1], tm=512)
            o = _swa(q, kv, attn_sinks[bl], B, S)
        h = _proj_res(o, w_mix_out, h, gate, S)

        gain, shift, scale, gate = vecs(2)
        last = layer == depth - 1
        jobs = ([whole(w_kv)] if layer == n_a - 1 else []) + ([] if last else ffn_jobs(layer + 1, 0))
        h, cast = _ffn(h, gain, shift, scale, gate, *ffn_w, S, cast_jobs=jobs,
                       final_gain=final_gain[None] if last else None)

        if layer == n_a - 1:
            w_kv16, cast = cast[0], cast[1:]
            kv = _norm_proj(h, kv_gain[None], mod_kv[:, 0], mod_kv[:, 1], w_kv16, S, BF16,
                            bias=b_kv[None], rope=rope, rot_cols=w_kv.shape[1] // 2, tm=512)
        ffn_w = cast

    return h.reshape(B, S, D)
```

```python
import functools

import numpy as np
import jax
import jax.numpy as jnp
from jax import lax
from jax.experimental import pallas as pl
from jax.experimental.pallas import tpu as pltpu

F32 = jnp.float32
BF16 = jnp.bfloat16

NORM_EPS = 1e-6
NEG_INF = -1e30
ROPE_THETA = 500000.0

LANES = 128
SUBLANES = 8
MIB = 1 << 20

HGRN_HEAD = 128
HGRN_CHUNK = 128
HEAD_DIM = 64
ROT_DIM = HEAD_DIM // 4
GROUP = 8
WINDOW = 128
N_SUBLAYERS = 3


def _params(semantics, vmem_mib):
    return pltpu.CompilerParams(dimension_semantics=semantics, vmem_limit_bytes=vmem_mib * MIB)


def _sigmoid(x):
    return 1.0 / (1.0 + jnp.exp(-x))


def _ada_norm(h, gain, shift, scale):
    y = h * lax.rsqrt(jnp.mean(h * h, axis=-1, keepdims=True) + NORM_EPS)
    return (y * gain) * (1.0 + scale) + shift


PACK_ROWS = 16


def _split_bf16x3(x):
    hi = x.astype(BF16)
    r1 = x - hi.astype(F32)
    mid = r1.astype(BF16)
    lo = (r1 - mid.astype(F32)).astype(BF16)
    return hi, mid, lo


def _partial_sums_to_inv_rms(inv_scr, d):
    ones = jnp.ones((LANES, LANES), BF16)
    ssq = None
    for part in _split_bf16x3(inv_scr[...]):
        s = jnp.dot(part, ones, preferred_element_type=F32)
        ssq = s if ssq is None else ssq + s
    inv_scr[...] = lax.rsqrt(ssq * (1.0 / d) + NORM_EPS)


def _ada_norm_into(u_ref, h_ref, gain_ref, shift_ref, scale_ref, inv_scr, zero_ref=None):
    n_rows, d = h_ref.shape
    n_cb = d // LANES

    def pass1(i, carry):
        r = pl.multiple_of(i * SUBLANES, SUBLANES)
        acc = None
        for cb in range(n_cb):
            xb = h_ref[pl.ds(r, SUBLANES), cb * LANES:(cb + 1) * LANES]
            acc = xb * xb if acc is None else acc + xb * xb
        inv_scr[pl.ds(r, SUBLANES), :] = acc
        return carry
    lax.fori_loop(0, n_rows // SUBLANES, pass1, 0, unroll=4)
    _partial_sums_to_inv_rms(inv_scr, d)

    mult = gain_ref[...] * (1.0 + scale_ref[...])
    shift = shift_ref[...]

    def pass2(i, carry):
        r = pl.multiple_of(i * PACK_ROWS, PACK_ROWS)
        rows = pl.ds(r, PACK_ROWS)
        inv = inv_scr[rows, :]
        for cb in range(n_cb):
            cols = slice(cb * LANES, (cb + 1) * LANES)
            u_ref[rows, cols] = ((h_ref[rows, cols] * inv) * mult[:, cols] + shift[:, cols]).astype(u_ref.dtype)
            if zero_ref is not None:
                zero_ref[rows, cols] = jnp.zeros((PACK_ROWS, LANES), zero_ref.dtype)
        return carry
    lax.fori_loop(0, n_rows // PACK_ROWS, pass2, 0, unroll=2)


def _mod_kernel(c_ref, w_ref, b_ref, o_ref):
    c = c_ref[...]
    cs = (c * _sigmoid(c)).astype(BF16)
    o_ref[...] = jnp.dot(cs, w_ref[...].astype(BF16), preferred_element_type=F32) + b_ref[...]


def _ada_mod(c_rows, w, b):
    L, D, N = w.shape
    tn = 1024
    return pl.pallas_call(
        _mod_kernel,
        grid=(L, N // tn),
        in_specs=[
            pl.BlockSpec((SUBLANES, D), lambda l, j: (0, 0)),
            pl.BlockSpec((None, D, tn), lambda l, j: (l, 0, j)),
            pl.BlockSpec((None, 1, tn), lambda l, j: (l, 0, j)),
        ],
        out_specs=pl.BlockSpec((None, SUBLANES, tn), lambda l, j: (l, 0, j)),
        out_shape=jax.ShapeDtypeStruct((L, SUBLANES, N), F32),
        compiler_params=_params(("arbitrary", "arbitrary"), 40),
        name="ada_mod",
    )(c_rows, w, b.reshape(L, 1, N))


def _cast_specs(jobs, n_steps, step_of):
    in_specs, out_specs, out_shape = [], [], []
    for src, lead, block_rows, width, col_block in jobs:
        n_rows = src.shape[len(lead)]
        n_blocks = n_rows // block_rows
        assert n_rows % block_rows == 0 and n_blocks <= n_steps and block_rows % PACK_ROWS == 0

        def block_of(*idx, n_blocks=n_blocks):
            return jnp.minimum(step_of(*idx), n_blocks - 1)

        in_specs.append(pl.BlockSpec((None,) * len(lead) + (block_rows, width),
                                     lambda *idx, f=block_of, lead=lead, cb=col_block: lead + (f(*idx), cb)))
        out_specs.append(pl.BlockSpec((block_rows, width), lambda *idx, f=block_of: (f(*idx), 0)))
        out_shape.append(jax.ShapeDtypeStruct((n_rows, width), BF16))
    return in_specs, out_specs, out_shape


def _cast_kernel(*refs):
    n = len(refs) // 2
    for src_ref, dst_ref in zip(refs[:n], refs[n:]):
        dst_ref[...] = src_ref[...].astype(dst_ref.dtype)


def _cast_weights(jobs, n_steps):
    in_specs, out_specs, out_shape = _cast_specs(jobs, n_steps, lambda s: s)
    return pl.pallas_call(
        _cast_kernel,
        grid=(n_steps,),
        in_specs=in_specs,
        out_specs=out_specs,
        out_shape=out_shape,
        compiler_params=_params(("arbitrary",), 32),
        name="cast_weights",
    )(*[job[0] for job in jobs])


def _ffn_kernel(*refs, n_f, tail, final_norm, n_cast):
    refs = list(refs)
    h_ref, gain_ref, shift_ref, scale_ref, gate_ref, wa_ref, wb_ref, wo_ref = refs[:8]
    pos = 8
    fg_ref = None
    if final_norm:
        fg_ref = refs[pos]
        pos += 1
    cast_src = refs[pos:pos + n_cast]
    o_ref = refs[pos + n_cast]
    cast_dst = refs[pos + n_cast + 1:pos + 2 * n_cast + 1]
    u_scr, inv_scr = refs[pos + 2 * n_cast + 1:]
    j = pl.program_id(1)

    @pl.when(j == 0)
    def _():
        _ada_norm_into(u_scr, h_ref, gain_ref, shift_ref, scale_ref, inv_scr, zero_ref=o_ref)

    def accumulate(cols):
        u = u_scr[...]
        a = jnp.dot(u, wa_ref[:, :cols], preferred_element_type=F32)
        b = jnp.dot(u, wb_ref[:, :cols], preferred_element_type=F32)
        g = ((a * _sigmoid(a)) * b).astype(BF16)
        o_ref[...] += jnp.dot(g, wo_ref[:cols, :], preferred_element_type=F32)
        for src_ref, dst_ref in zip(cast_src, cast_dst):
            dst_ref[...] = src_ref[...].astype(dst_ref.dtype)

    tf = wa_ref.shape[1]
    if tail == tf:
        accumulate(tf)
    else:
        pl.when(j < n_f - 1)(lambda: accumulate(tf))
        pl.when(j == n_f - 1)(lambda: accumulate(tail))

    @pl.when(j == n_f - 1)
    def _():
        n_rows, d = o_ref.shape
        n_cb = d // LANES
        half_gate = 0.5 * gate_ref[...]

        def residual(i, carry):
            r = pl.multiple_of(i * SUBLANES, SUBLANES)
            rows = pl.ds(r, SUBLANES)
            acc = None
            for cb in range(n_cb):
                cols = slice(cb * LANES, (cb + 1) * LANES)
                out = h_ref[rows, cols] + half_gate[:, cols] * o_ref[rows, cols]
                o_ref[rows, cols] = out
                if final_norm:
                    acc = out * out if acc is None else acc + out * out
            if final_norm:
                inv_scr[rows, :] = acc
            return carry
        lax.fori_loop(0, n_rows // SUBLANES, residual, 0, unroll=4)

        if final_norm:
            _partial_sums_to_inv_rms(inv_scr, d)
            final_gain = fg_ref[...]

            def normalise(i, carry):
                r = pl.multiple_of(i * SUBLANES, SUBLANES)
                rows = pl.ds(r, SUBLANES)
                inv = inv_scr[rows, :]
                for cb in range(n_cb):
                    cols = slice(cb * LANES, (cb + 1) * LANES)
                    o_ref[rows, cols] = (o_ref[rows, cols] * inv) * final_gain[:, cols]
                return carry
            lax.fori_loop(0, n_rows // SUBLANES, normalise, 0, unroll=4)


def _ffn(h, gain, shift, scale, gate, w_a, w_b, w_out, seq, final_gain=None, cast_jobs=(), tm=1024, tf=512):
    T, D = h.shape
    d_ff = w_out.shape[0]
    n_f = pl.cdiv(d_ff, tf)
    tail = d_ff - (n_f - 1) * tf
    assert tail % LANES == 0
    per_b = seq // tm
    final_norm = final_gain is not None
    vec = pl.BlockSpec((1, D), lambda i, j: (0, 0))
    bvec = pl.BlockSpec((None, 1, D), lambda i, j: (i // per_b, 0, 0))
    in_specs = [
        pl.BlockSpec((tm, D), lambda i, j: (i, 0)),
        vec, bvec, bvec, bvec,
        pl.BlockSpec((D, tf), lambda i, j: (0, j)),
        pl.BlockSpec((D, tf), lambda i, j: (0, j)),
        pl.BlockSpec((tf, D), lambda i, j: (j, 0)),
    ]
    args = [h, gain, shift, scale, gate, w_a, w_b, w_out]
    if final_norm:
        in_specs.append(vec)
        args.append(final_gain)
    out_specs = [pl.BlockSpec((tm, D), lambda i, j: (i, 0))]
    out_shape = [jax.ShapeDtypeStruct((T, D), F32)]
    cast_in, cast_out, cast_shape = _cast_specs(cast_jobs, (T // tm) * n_f, lambda i, j: i * n_f + j)
    in_specs += cast_in
    args += [job[0] for job in cast_jobs]
    out_specs += cast_out
    out_shape += cast_shape
    outs = pl.pallas_call(
        functools.partial(_ffn_kernel, n_f=n_f, tail=tail, final_norm=final_norm, n_cast=len(cast_jobs)),
        grid=(T // tm, n_f),
        in_specs=in_specs,
        out_specs=out_specs,
        out_shape=out_shape,
        scratch_shapes=[pltpu.VMEM((tm, D), BF16), pltpu.VMEM((tm, LANES), F32)],
        compiler_params=_params(("arbitrary", "arbitrary"), 60),
        name="ffn",
    )(*args)
    return outs[0], outs[1:]


def _rotate_block(z, cos, sin_lo, sin_hi):
    half = ROT_DIM // 2
    up = pltpu.roll(z, LANES - half, axis=1)
    down = pltpu.roll(z, half, axis=1)
    return z * cos + up * sin_lo + down * sin_hi


def _norm_proj_kernel(*refs, has_bias, rot_cols):
    refs = list(refs)
    h_ref, gain_ref, shift_ref, scale_ref, w_ref = refs[:5]
    pos = 5
    bias_ref = None
    if has_bias:
        bias_ref = refs[pos]
        pos += 1
    if rot_cols:
        cos_ref, slo_ref, shi_ref = refs[pos:pos + 3]
        pos += 3
    o_ref, u_scr, inv_scr = refs[pos:pos + 3]

    @pl.when(pl.program_id(1) == 0)
    def _():
        _ada_norm_into(u_scr, h_ref, gain_ref, shift_ref, scale_ref, inv_scr)

    z = jnp.dot(u_scr[...], w_ref[...], preferred_element_type=F32)
    if has_bias:
        z = z + bias_ref[...]
    if rot_cols:
        cos, slo, shi = cos_ref[...], slo_ref[...], shi_ref[...]
        tn = z.shape[1]
        blocks = []
        for cb in range(tn // LANES):
            zb = z[:, cb * LANES:(cb + 1) * LANES]
            if cb * LANES < rot_cols:
                zb = _rotate_block(zb, cos, slo, shi)
            blocks.append(zb)
        z = jnp.concatenate(blocks, axis=1) if len(blocks) > 1 else blocks[0]
    o_ref[...] = z.astype(o_ref.dtype)


def _norm_proj(h, gain, shift, scale, w, seq, out_dtype, bias=None, rope=None, rot_cols=0, tm=1024, tn=None):
    T, D = h.shape
    N = w.shape[1]
    tn = N if tn is None else tn
    per_b = seq // tm
    vec = pl.BlockSpec((1, D), lambda i, j: (0, 0))
    bvec = pl.BlockSpec((None, 1, D), lambda i, j: (i // per_b, 0, 0))
    in_specs = [pl.BlockSpec((tm, D), lambda i, j: (i, 0)), vec, bvec, bvec,
                pl.BlockSpec((D, tn), lambda i, j: (0, j))]
    args = [h, gain, shift, scale, w]
    if bias is not None:
        in_specs.append(pl.BlockSpec((1, tn), lambda i, j: (0, j)))
        args.append(bias)
    if rot_cols:
        assert tn == N and rot_cols % LANES == 0
        tab = pl.BlockSpec((tm, LANES), lambda i, j: (i % per_b, 0))
        in_specs += [tab, tab, tab]
        args += list(rope)
    return pl.pallas_call(
        functools.partial(_norm_proj_kernel, has_bias=bias is not None, rot_cols=rot_cols),
        grid=(T // tm, N // tn),
        in_specs=in_specs,
        out_specs=pl.BlockSpec((tm, tn), lambda i, j: (i, j)),
        out_shape=jax.ShapeDtypeStruct((T, N), out_dtype),
        scratch_shapes=[pltpu.VMEM((tm, D), BF16), pltpu.VMEM((tm, LANES), F32)],
        compiler_params=_params(("arbitrary", "arbitrary"), 48),
        name="norm_proj",
    )(*args)


def _proj_res_kernel(x_ref, w_ref, h_ref, gate_ref, o_ref):
    y = jnp.dot(x_ref[...], w_ref[...], preferred_element_type=F32)
    o_ref[...] = h_ref[...] + gate_ref[...] * y


def _proj_res(x, w, h, gate, seq, tm=512):
    T, K = x.shape
    N = w.shape[1]
    per_b = seq // tm
    return pl.pallas_call(
        _proj_res_kernel,
        grid=(T // tm,),
        in_specs=[
            pl.BlockSpec((tm, K), lambda i: (i, 0)),
            pl.BlockSpec((K, N), lambda i: (0, 0)),
            pl.BlockSpec((tm, N), lambda i: (i, 0)),
            pl.BlockSpec((None, 1, N), lambda i: (i // per_b, 0, 0)),
        ],
        out_specs=pl.BlockSpec((tm, N), lambda i: (i, 0)),
        out_shape=jax.ShapeDtypeStruct((T, N), F32),
        compiler_params=_params(("arbitrary",), 48),
        name="proj_res",
    )(x, w, h, gate)


def _hgrn_scan_kernel(zq_ref, zf_ref, zi_ref, zg_ref, lbl_ref, hg_ref, o_ref,
                      st_ref, q_scr, b_scr, c_scr, v_scr, y_scr, bc_scr, cc_scr, mask_scr, spread_scr, tril_scr,
                      *, layer, n_chunks, n_heads):
    C = HGRN_CHUNK
    K = HGRN_HEAD
    levels = []
    m = C // 2
    while m >= SUBLANES:
        levels.append(m)
        m //= 2

    @pl.when(pl.program_id(2) == 0)
    def _():
        st_ref[...] = jnp.zeros_like(st_ref)
        row_c = lax.broadcasted_iota(jnp.int32, (C, C), 0)
        col_c = lax.broadcasted_iota(jnp.int32, (C, C), 1)
        tril = (row_c >= col_c).astype(BF16)
        tril_scr[...] = jnp.concatenate([tril, tril, tril], axis=1)
        xor_c = row_c ^ col_c
        mask_scr[0] = ((xor_c < SUBLANES) & (col_c <= row_c)).astype(F32)
        for li, m in enumerate(levels):
            if li:
                mask_scr[li] = (xor_c < 2 * m).astype(F32)
        src = lax.shift_right_logical(lax.broadcasted_iota(jnp.int32, (SUBLANES * K, C), 0), K.bit_length() - 1)
        dst = lax.broadcasted_iota(jnp.int32, (SUBLANES * K, C), 1) & (SUBLANES - 1)
        spread_scr[...] = (src == dst).astype(BF16)

    def prepare(hh, r0):
        lanes = slice(hh * K, (hh + 1) * K)
        rows = pl.ds(r0, C)
        logits = lbl_ref[:, lanes]
        e = jnp.exp(logits - jnp.max(logits, axis=0, keepdims=True))
        lb = jnp.sum(e[:layer + 1], axis=0, keepdims=True) / jnp.sum(e, axis=0, keepdims=True)

        zq = zq_ref[rows, lanes]
        q_scr[rows, lanes] = zq * _sigmoid(zq)
        forget = lb + (1.0 - lb) * _sigmoid(zf_ref[rows, lanes])
        v_scr[rows, lanes] = zi_ref[rows, lanes].astype(BF16)
        parts = jnp.concatenate(_split_bf16x3(jnp.log2(forget)), axis=0)
        b = jnp.dot(tril_scr[...], parts, preferred_element_type=F32)
        b_scr[rows, lanes] = b
        c_scr[rows, lanes] = b - jnp.log2(1.0 - forget)

    def mix(hh, r0):
        lanes = slice(hh * K, (hh + 1) * K)
        rows = pl.ds(r0, C)
        q = q_scr[rows, lanes]
        b = b_scr[rows, lanes]
        c = c_scr[rows, lanes]
        v16 = v_scr[rows, lanes]
        st = st_ref[hh]
        bc_scr[hh] = b
        cc_scr[hh] = c

        qd = (q * jnp.exp2(b)).astype(BF16)
        out = lax.dot_general(qd, st.astype(BF16), (((1,), (1,)), ((), ())), preferred_element_type=F32)
        b_last = bc_scr[hh, pl.ds(C - 1, 1), :]
        kd = jnp.exp2(b_last - c).astype(BF16)
        st_ref[hh] = st * jnp.exp2(b_last) + lax.dot_general(
            v16, kd, (((0,), (0,)), ((), ())), preferred_element_type=F32)

        w_rows = []
        for blk in range(C // SUBLANES):
            rb = blk * SUBLANES
            qb = q[rb:rb + SUBLANES]
            bb = b[rb:rb + SUBLANES]
            w_s = []
            for s in range(SUBLANES):
                rel = jnp.minimum(bb - cc_scr[hh, pl.ds(rb + s, 1), :], 0.0)
                w_s.append(qb * jnp.exp2(rel))
            w_rows.append(jnp.concatenate(w_s, axis=1))
        w_all = jnp.concatenate(w_rows, axis=0).astype(BF16)
        p = jnp.dot(w_all, spread_scr[...], preferred_element_type=F32) * mask_scr[0]

        n_groups = C // SUBLANES
        zero_group = jnp.zeros((SUBLANES, K), F32)
        for li, m in enumerate(levels):
            qt, kt = [], []
            for g in range(n_groups):
                rg = g * SUBLANES
                rows_g = slice(rg, rg + SUBLANES)
                anchor = (rg // (2 * m)) * 2 * m + m - 1
                ba = bc_scr[hh, pl.ds(anchor, 1), :]
                if rg & m:
                    qt.append(q[rows_g] * jnp.exp2(b[rows_g] - ba))
                    kt.append(zero_group)
                else:
                    qt.append(zero_group)
                    kt.append(jnp.exp2(ba - c[rows_g]))
            a_m = lax.dot_general(jnp.concatenate(qt, axis=0).astype(BF16),
                                  jnp.concatenate(kt, axis=0).astype(BF16),
                                  (((1,), (1,)), ((), ())), preferred_element_type=F32)
            p = p + (a_m if 2 * m == C else a_m * mask_scr[li])
        y_scr[rows, lanes] = out + jnp.dot(p.astype(BF16), v16, preferred_element_type=F32)

    def finish(hh, r0):
        lanes = slice(hh * K, (hh + 1) * K)
        rows = pl.ds(r0, C)
        out = y_scr[rows, lanes]
        y = out * lax.rsqrt(jnp.mean(out * out, axis=-1, keepdims=True) + NORM_EPS) * hg_ref[:, lanes]
        o_ref[rows, lanes] = (y * _sigmoid(zg_ref[rows, lanes])).astype(o_ref.dtype)

    def run(stage, ci):
        r0 = ci * C if isinstance(ci, int) else pl.multiple_of(ci * C, C)
        for hh in range(n_heads):
            stage(hh, r0)

    run(prepare, 0)
    run(mix, 0)
    run(prepare, 1)

    def trip(ci, carry):
        run(finish, ci - 1)
        run(mix, ci)
        run(prepare, jnp.minimum(ci + 1, n_chunks - 1))
        return carry
    lax.fori_loop(1, n_chunks, trip, 0)
    run(finish, n_chunks - 1)


def _hgrn_scan(z, lb_logits, head_gain, batch, seq, layer, ts=1024, hpb=4):
    assert HGRN_CHUNK == HGRN_HEAD
    T, D4 = z.shape
    D = D4 // 4
    n_s = seq // ts
    hw = hpb * HGRN_HEAD
    n_hb = D // hw

    def zspec(group):
        return pl.BlockSpec((ts, hw), lambda bi, hi, si: (bi * n_s + si, group * n_hb + hi))

    n_layers = lb_logits.shape[0]
    n_levels = (HGRN_CHUNK // SUBLANES).bit_length() - 1
    return pl.pallas_call(
        functools.partial(_hgrn_scan_kernel, layer=layer, n_chunks=ts // HGRN_CHUNK, n_heads=hpb),
        grid=(batch, n_hb, n_s),
        in_specs=[zspec(0), zspec(1), zspec(2), zspec(3),
                  pl.BlockSpec((n_layers, hw), lambda bi, hi, si: (0, hi)),
                  pl.BlockSpec((1, hw), lambda bi, hi, si: (0, hi))],
        out_specs=pl.BlockSpec((ts, hw), lambda bi, hi, si: (bi * n_s + si, hi)),
        out_shape=jax.ShapeDtypeStruct((T, D), BF16),
        scratch_shapes=[pltpu.VMEM((hpb, HGRN_HEAD, HGRN_HEAD), F32),
                        pltpu.VMEM((ts, hw), F32),
                        pltpu.VMEM((ts, hw), F32),
                        pltpu.VMEM((ts, hw), F32),
                        pltpu.VMEM((ts, hw), BF16),
                        pltpu.VMEM((ts, hw), F32),
                        pltpu.VMEM((hpb, HGRN_CHUNK, HGRN_HEAD), F32),
                        pltpu.VMEM((hpb, HGRN_CHUNK, HGRN_HEAD), F32),
                        pltpu.VMEM((n_levels, HGRN_CHUNK, HGRN_CHUNK), F32),
                        pltpu.VMEM((SUBLANES * HGRN_HEAD, HGRN_CHUNK), BF16),
                        pltpu.VMEM((HGRN_CHUNK, 3 * HGRN_CHUNK), BF16)],
        compiler_params=_params(("arbitrary", "arbitrary", "arbitrary"), 32),
        name="hgrn_scan",
    )(z, z, z, z, lb_logits, head_gain)


def _swa_kernel(sink_ref, q_ref, kc_ref, kp_ref, vc_ref, vp_ref, o_ref, bias_scr, *, n_blocks):
    W = WINDOW
    pair = pl.program_id(1)
    first_tile = pl.program_id(2) == 0
    lane_k = lax.broadcasted_iota(jnp.int32, (2 * W, LANES), 1)
    qi = lax.broadcasted_iota(jnp.int32, (W, 2 * W), 0)
    kj = lax.broadcasted_iota(jnp.int32, (W, 2 * W), 1)
    in_window = (kj > qi) & (kj <= qi + W)
    bias_scr[0] = jnp.where(in_window & (kj >= jnp.where(first_tile, W, 0)), 0.0, NEG_INF)
    bias_scr[1] = jnp.where(in_window, 0.0, NEG_INF)

    for n in range(n_blocks):
        rows = slice(n * W, (n + 1) * W)
        q2 = jnp.concatenate([q_ref[rows, j * LANES:(j + 1) * LANES] for j in range(GROUP)], axis=0)
        q2 = q2 * jnp.asarray(HEAD_DIM ** -0.5, q2.dtype)
        if n == 0:
            k_prev, v_prev = kp_ref[...], vp_ref[...]
        else:
            k_prev, v_prev = kc_ref[(n - 1) * W:n * W, :], vc_ref[(n - 1) * W:n * W, :]
        kk = jnp.concatenate([k_prev, kc_ref[rows, :]], axis=0)
        vv = jnp.concatenate([v_prev, vc_ref[rows, :]], axis=0)
        zero = jnp.zeros_like(kk)
        k_bd = jnp.concatenate([jnp.where(lane_k < HEAD_DIM, kk, zero),
                                jnp.where(lane_k >= HEAD_DIM, kk, zero)], axis=0)
        v_bd = jnp.concatenate([jnp.where(lane_k < HEAD_DIM, vv, zero),
                                jnp.where(lane_k >= HEAD_DIM, vv, zero)], axis=0)
        s = lax.dot_general(q2, k_bd, (((1,), (1,)), ((), ())), preferred_element_type=F32)
        bias = bias_scr[0 if n == 0 else 1]
        p_rows = []
        for j in range(GROUP):
            p_cols = []
            for gl in range(2):
                sub = s[j * W:(j + 1) * W, gl * 2 * W:(gl + 1) * 2 * W] + bias
                sink = sink_ref[(2 * pair + gl) * GROUP + j]
                mx = jnp.maximum(jnp.max(sub, axis=-1, keepdims=True), sink)
                p = jnp.exp(sub - mx)
                den = jnp.sum(p, axis=-1, keepdims=True) + jnp.exp(sink - mx)
                p_cols.append((p * (1.0 / den)).astype(BF16))
            p_rows.append(jnp.concatenate(p_cols, axis=1))
        probs = jnp.concatenate(p_rows, axis=0)
        out = jnp.dot(probs, v_bd, preferred_element_type=F32)
        for j in range(GROUP):
            o_ref[rows, j * LANES:(j + 1) * LANES] = out[j * W:(j + 1) * W].astype(o_ref.dtype)


def _swa(q, kv, sinks, batch, seq, tq=1024):
    T, DQ = q.shape
    n_pairs = kv.shape[1] // (2 * LANES)
    qw = DQ // n_pairs
    n_t = seq // tq
    wpt = tq // WINDOW

    def cur(col0):
        return pl.BlockSpec((tq, LANES), lambda bi, pi, ti: (bi * n_t + ti, col0 + pi))

    def prev(col0):
        return pl.BlockSpec(
            (WINDOW, LANES),
            lambda bi, pi, ti: (jnp.maximum((bi * n_t + ti) * wpt - 1, 0), col0 + pi))

    return pl.pallas_call(
        functools.partial(_swa_kernel, n_blocks=wpt),
        grid=(batch, n_pairs, n_t),
        in_specs=[pl.BlockSpec(memory_space=pltpu.SMEM),
                  pl.BlockSpec((tq, qw), lambda bi, pi, ti: (bi * n_t + ti, pi)),
                  cur(0), prev(0), cur(n_pairs), prev(n_pairs)],
        out_specs=pl.BlockSpec((tq, qw), lambda bi, pi, ti: (bi * n_t + ti, pi)),
        out_shape=jax.ShapeDtypeStruct((T, DQ), BF16),
        scratch_shapes=[pltpu.VMEM((2, WINDOW, 2 * WINDOW), F32)],
        compiler_params=_params(("arbitrary", "arbitrary", "arbitrary"), 40),
        name="swa",
    )(sinks, q, kv, kv, kv, kv)


def _rope_lane_tables(seq):
    half = ROT_DIM // 2
    inv_freq = jnp.power(jnp.float32(ROPE_THETA), -jnp.arange(0, ROT_DIM, 2, dtype=F32) / ROT_DIM)
    ang = jnp.arange(seq, dtype=F32)[:, None] * inv_freq[None, :]
    sin, cos = jnp.sin(ang), jnp.cos(ang)
    ones = jnp.ones((seq, HEAD_DIM - ROT_DIM), F32)
    zeros_rest = jnp.zeros((seq, HEAD_DIM - ROT_DIM), F32)
    zeros_half = jnp.zeros((seq, half), F32)
    cos_h = jnp.concatenate([cos, cos, ones], axis=1)
    lo_h = jnp.concatenate([-sin, zeros_half, zeros_rest], axis=1)
    hi_h = jnp.concatenate([zeros_half, sin, zeros_rest], axis=1)
    reps = LANES // HEAD_DIM
    return tuple(jnp.tile(t, (1, reps)) for t in (cos_h, lo_h, hi_h))


def _pair_major_heads(n_q_heads):
    order = []
    for p in range(n_q_heads // (2 * GROUP)):
        for j in range(GROUP):
            for gl in range(2):
                order.append((2 * p + gl) * GROUP + j)
    return np.asarray(order, dtype=np.int32)


def kernel(x, c, norm_gain, w_ada, b_ada, w_ffn_in, w_ffn_out, w_hgrn_in, hgrn_lb_logits, hgrn_head_gain,
           w_hgrn_out, kv_gain, w_ada_kv, b_ada_kv, w_kv, b_kv, w_q, b_q, attn_sinks, w_attn_out, final_gain):
    B, S, D = x.shape
    T = B * S
    depth = w_ada.shape[0]
    n_a = w_hgrn_in.shape[0]
    d_ff = w_ffn_out.shape[2]
    n_q_heads = w_q.shape[2] // HEAD_DIM

    c_rows = jnp.pad(c, ((0, SUBLANES - B), (0, 0)))
    mod = _ada_mod(c_rows, w_ada, b_ada)[:, :B].reshape(depth, B, N_SUBLAYERS, 3, 1, D)
    mod_kv = _ada_mod(c_rows, w_ada_kv[None], b_ada_kv[None])[0, :B].reshape(B, 2, 1, D)

    heads = _pair_major_heads(n_q_heads)
    rope = _rope_lane_tables(S)

    def ffn_jobs(layer, k, rows=PACK_ROWS):
        return [(w_ffn_in, (layer, k), rows, d_ff, 0), (w_ffn_in, (layer, k), rows, d_ff, 1),
                (w_ffn_out, (layer, k), 2 * rows, D, 0)]

    def whole(w, lead=()):
        return (w, lead, PACK_ROWS, w.shape[-1], 0)

    def mixer_jobs(layer):
        if layer < n_a:
            return [whole(w_hgrn_in, (layer,)), whole(w_hgrn_out, (layer,))]
        bl = layer - n_a
        wq = w_q[bl].reshape(D, n_q_heads, HEAD_DIM)[:, heads].reshape(D, -1)
        wo = w_attn_out[bl].reshape(n_q_heads, HEAD_DIM, D)[heads].reshape(-1, D)
        return [whole(wq), whole(wo)]

    first = ffn_jobs(0, 0, rows=4 * PACK_ROWS)
    ffn_w = _cast_weights(first, max(job[0].shape[-2] // job[2] for job in first))

    h = x.reshape(T, D)
    kv = None
    for layer in range(depth):
        def vecs(sub):
            return (norm_gain[layer, sub][None], mod[layer, :, sub, 0], mod[layer, :, sub, 1], mod[layer, :, sub, 2])

        gain, shift, scale, gate = vecs(0)
        jobs = mixer_jobs(layer) + ffn_jobs(layer, 1)
        h, cast = _ffn(h, gain, shift, scale, gate, *ffn_w, S, cast_jobs=jobs)
        w_mix_in, w_mix_out, ffn_w = cast[0], cast[1], cast[2:]

        gain, shift, scale, gate = vecs(1)
        if layer < n_a:
            z = _norm_proj(h, gain, shift, scale, w_mix_in, S, F32, tn=1024)
            o = _hgrn_scan(z, hgrn_lb_logits, hgrn_head_gain[layer][None], B, S, layer)
        else:
            bl = layer - n_a
            bq = b_q[bl].reshape(n_q_heads, HEAD_DIM)[heads].reshape(1, -1)
            q = _norm_proj(h, gain, shift, scale, w_mix_in, S, BF16, bias=bq, rope=rope,
                           rot_cols=w_mix_in.shape[1], tm=512)
            o = _swa(q, kv, attn_sinks[bl], B, S)
        h = _proj_res(o, w_mix_out, h, gate, S)

        gain, shift, scale, gate = vecs(2)
        last = layer == depth - 1
        jobs = ([whole(w_kv)] if layer == n_a - 1 else []) + ([] if last else ffn_jobs(layer + 1, 0))
        h, cast = _ffn(h, gain, shift, scale, gate, *ffn_w, S, cast_jobs=jobs,
                       final_gain=final_gain[None] if last else None)

        if layer == n_a - 1:
            w_kv16, cast = cast[0], cast[1:]
            kv = _norm_proj(h, kv_gain[None], mod_kv[:, 0], mod_kv[:, 1], w_kv16, S, BF16,
                            bias=b_kv[None], rope=rope, rot_cols=w_kv.shape[1] // 2, tm=512)
        ffn_w = cast

    return h.reshape(B, S, D)
```

```python
import functools

import numpy as np
import jax
import jax.numpy as jnp
from jax import lax
from jax.experimental import pallas as pl
from jax.experimental.pallas import tpu as pltpu

F32 = jnp.float32
BF16 = jnp.bfloat16

NORM_EPS = 1e-6
NEG_INF = -1e30
ROPE_THETA = 500000.0

LANES = 128
SUBLANES = 8
MIB = 1 << 20

HGRN_HEAD = 128
HGRN_CHUNK = 128
HEAD_DIM = 64
ROT_DIM = HEAD_DIM // 4
GROUP = 8
WINDOW = 128
N_SUBLAYERS = 3


PACK_ROWS = 16

VMEM_MIB = dict(ada_mod=40, cast_weights=32, ffn=60, norm_proj=48, proj_res=48, hgrn_scan=32, swa=40)


def _params(name, n_grid_axes):
    return pltpu.CompilerParams(dimension_semantics=("arbitrary",) * n_grid_axes,
                                vmem_limit_bytes=VMEM_MIB[name] * MIB)


def _sigmoid(x):
    return 1.0 / (1.0 + jnp.exp(-x))


def _split_bf16x3(x):
    hi = x.astype(BF16)
    r1 = x - hi.astype(F32)
    mid = r1.astype(BF16)
    lo = (r1 - mid.astype(F32)).astype(BF16)
    return hi, mid, lo


def _partial_sums_to_inv_rms(inv_scr, d):
    ones = jnp.ones((LANES, LANES), BF16)
    ssq = None
    for part in _split_bf16x3(inv_scr[...]):
        s = jnp.dot(part, ones, preferred_element_type=F32)
        ssq = s if ssq is None else ssq + s
    inv_scr[...] = lax.rsqrt(ssq * (1.0 / d) + NORM_EPS)


def _ada_norm_into(u_ref, h_ref, gain_ref, shift_ref, scale_ref, inv_scr, zero_ref=None):
    n_rows, d = h_ref.shape
    n_cb = d // LANES

    def pass1(i, carry):
        r = pl.multiple_of(i * SUBLANES, SUBLANES)
        acc = None
        for cb in range(n_cb):
            xb = h_ref[pl.ds(r, SUBLANES), cb * LANES:(cb + 1) * LANES]
            acc = xb * xb if acc is None else acc + xb * xb
        inv_scr[pl.ds(r, SUBLANES), :] = acc
        return carry
    lax.fori_loop(0, n_rows // SUBLANES, pass1, 0, unroll=4)
    _partial_sums_to_inv_rms(inv_scr, d)

    mult = gain_ref[...] * (1.0 + scale_ref[...])
    shift = shift_ref[...]

    def pass2(i, carry):
        r = pl.multiple_of(i * PACK_ROWS, PACK_ROWS)
        rows = pl.ds(r, PACK_ROWS)
        inv = inv_scr[rows, :]
        for cb in range(n_cb):
            cols = slice(cb * LANES, (cb + 1) * LANES)
            u_ref[rows, cols] = ((h_ref[rows, cols] * inv) * mult[:, cols] + shift[:, cols]).astype(u_ref.dtype)
            if zero_ref is not None:
                zero_ref[rows, cols] = jnp.zeros((PACK_ROWS, LANES), zero_ref.dtype)
        return carry
    lax.fori_loop(0, n_rows // PACK_ROWS, pass2, 0, unroll=2)


def _mod_kernel(c_ref, w_ref, b_ref, o_ref):
    c = c_ref[...]
    cs = (c * _sigmoid(c)).astype(BF16)
    o_ref[...] = jnp.dot(cs, w_ref[...].astype(BF16), preferred_element_type=F32) + b_ref[...]


def _ada_mod(c_rows, w, b):
    L, D, N = w.shape
    tn = 1024
    return pl.pallas_call(
        _mod_kernel,
        grid=(L, N // tn),
        in_specs=[
            pl.BlockSpec((SUBLANES, D), lambda l, j: (0, 0)),
            pl.BlockSpec((None, D, tn), lambda l, j: (l, 0, j)),
            pl.BlockSpec((None, 1, tn), lambda l, j: (l, 0, j)),
        ],
        out_specs=pl.BlockSpec((None, SUBLANES, tn), lambda l, j: (l, 0, j)),
        out_shape=jax.ShapeDtypeStruct((L, SUBLANES, N), F32),
        compiler_params=_params("ada_mod", 2),
        name="ada_mod",
    )(c_rows, w, b.reshape(L, 1, N))


def _cast_specs(jobs, n_steps, step_of):
    in_specs, out_specs, out_shape = [], [], []
    for src, lead, block_rows, width, col_block in jobs:
        n_rows = src.shape[len(lead)]
        n_blocks = n_rows // block_rows
        assert n_rows % block_rows == 0 and n_blocks <= n_steps and block_rows % PACK_ROWS == 0

        def block_of(*idx, n_blocks=n_blocks):
            return jnp.minimum(step_of(*idx), n_blocks - 1)

        in_specs.append(pl.BlockSpec((None,) * len(lead) + (block_rows, width),
                                     lambda *idx, f=block_of, lead=lead, cb=col_block: lead + (f(*idx), cb)))
        out_specs.append(pl.BlockSpec((block_rows, width), lambda *idx, f=block_of: (f(*idx), 0)))
        out_shape.append(jax.ShapeDtypeStruct((n_rows, width), BF16))
    return in_specs, out_specs, out_shape


def _cast_kernel(*refs):
    n = len(refs) // 2
    for src_ref, dst_ref in zip(refs[:n], refs[n:]):
        dst_ref[...] = src_ref[...].astype(dst_ref.dtype)


def _cast_weights(jobs, n_steps):
    in_specs, out_specs, out_shape = _cast_specs(jobs, n_steps, lambda s: s)
    return pl.pallas_call(
        _cast_kernel,
        grid=(n_steps,),
        in_specs=in_specs,
        out_specs=out_specs,
        out_shape=out_shape,
        compiler_params=_params("cast_weights", 1),
        name="cast_weights",
    )(*[job[0] for job in jobs])


def _ffn_kernel(*refs, n_f, tail, final_norm, n_cast):
    refs = list(refs)
    h_ref, gain_ref, shift_ref, scale_ref, gate_ref, wa_ref, wb_ref, wo_ref = refs[:8]
    pos = 8
    fg_ref = None
    if final_norm:
        fg_ref = refs[pos]
        pos += 1
    cast_src = refs[pos:pos + n_cast]
    o_ref = refs[pos + n_cast]
    cast_dst = refs[pos + n_cast + 1:pos + 2 * n_cast + 1]
    u_scr, inv_scr = refs[pos + 2 * n_cast + 1:]
    j = pl.program_id(1)

    @pl.when(j == 0)
    def _():
        _ada_norm_into(u_scr, h_ref, gain_ref, shift_ref, scale_ref, inv_scr, zero_ref=o_ref)

    def accumulate(cols):
        u = u_scr[...]
        a = jnp.dot(u, wa_ref[:, :cols], preferred_element_type=F32)
        b = jnp.dot(u, wb_ref[:, :cols], preferred_element_type=F32)
        g = ((a * _sigmoid(a)) * b).astype(BF16)
        o_ref[...] += jnp.dot(g, wo_ref[:cols, :], preferred_element_type=F32)
        for src_ref, dst_ref in zip(cast_src, cast_dst):
            dst_ref[...] = src_ref[...].astype(dst_ref.dtype)

    tf = wa_ref.shape[1]
    if tail == tf:
        accumulate(tf)
    else:
        pl.when(j < n_f - 1)(lambda: accumulate(tf))
        pl.when(j == n_f - 1)(lambda: accumulate(tail))

    @pl.when(j == n_f - 1)
    def _():
        n_rows, d = o_ref.shape
        n_cb = d // LANES
        half_gate = 0.5 * gate_ref[...]

        def residual(i, carry):
            r = pl.multiple_of(i * SUBLANES, SUBLANES)
            rows = pl.ds(r, SUBLANES)
            acc = None
            for cb in range(n_cb):
                cols = slice(cb * LANES, (cb + 1) * LANES)
                out = h_ref[rows, cols] + half_gate[:, cols] * o_ref[rows, cols]
                o_ref[rows, cols] = out
                if final_norm:
                    acc = out * out if acc is None else acc + out * out
            if final_norm:
                inv_scr[rows, :] = acc
            return carry
        lax.fori_loop(0, n_rows // SUBLANES, residual, 0, unroll=4)

        if final_norm:
            _partial_sums_to_inv_rms(inv_scr, d)
            final_gain = fg_ref[...]

            def normalise(i, carry):
                r = pl.multiple_of(i * SUBLANES, SUBLANES)
                rows = pl.ds(r, SUBLANES)
                inv = inv_scr[rows, :]
                for cb in range(n_cb):
                    cols = slice(cb * LANES, (cb + 1) * LANES)
                    o_ref[rows, cols] = (o_ref[rows, cols] * inv) * final_gain[:, cols]
                return carry
            lax.fori_loop(0, n_rows // SUBLANES, normalise, 0, unroll=4)


def _ffn(h, gain, shift, scale, gate, w_a, w_b, w_out, seq, final_gain=None, cast_jobs=(), tm=1024, tf=512):
    T, D = h.shape
    d_ff = w_out.shape[0]
    n_f = pl.cdiv(d_ff, tf)
    tail = d_ff - (n_f - 1) * tf
    assert tail % LANES == 0
    per_b = seq // tm
    final_norm = final_gain is not None
    vec = pl.BlockSpec((1, D), lambda i, j: (0, 0))
    bvec = pl.BlockSpec((None, 1, D), lambda i, j: (i // per_b, 0, 0))
    in_specs = [
        pl.BlockSpec((tm, D), lambda i, j: (i, 0)),
        vec, bvec, bvec, bvec,
        pl.BlockSpec((D, tf), lambda i, j: (0, j)),
        pl.BlockSpec((D, tf), lambda i, j: (0, j)),
        pl.BlockSpec((tf, D), lambda i, j: (j, 0)),
    ]
    args = [h, gain, shift, scale, gate, w_a, w_b, w_out]
    if final_norm:
        in_specs.append(vec)
        args.append(final_gain)
    out_specs = [pl.BlockSpec((tm, D), lambda i, j: (i, 0))]
    out_shape = [jax.ShapeDtypeStruct((T, D), F32)]
    cast_in, cast_out, cast_shape = _cast_specs(cast_jobs, (T // tm) * n_f, lambda i, j: i * n_f + j)
    in_specs += cast_in
    args += [job[0] for job in cast_jobs]
    out_specs += cast_out
    out_shape += cast_shape
    outs = pl.pallas_call(
        functools.partial(_ffn_kernel, n_f=n_f, tail=tail, final_norm=final_norm, n_cast=len(cast_jobs)),
        grid=(T // tm, n_f),
        in_specs=in_specs,
        out_specs=out_specs,
        out_shape=out_shape,
        scratch_shapes=[pltpu.VMEM((tm, D), BF16), pltpu.VMEM((tm, LANES), F32)],
        compiler_params=_params("ffn", 2),
        name="ffn",
    )(*args)
    return outs[0], outs[1:]


def _rotate_block(z, cos, sin_lo, sin_hi):
    half = ROT_DIM // 2
    up = pltpu.roll(z, LANES - half, axis=1)
    down = pltpu.roll(z, half, axis=1)
    return z * cos + up * sin_lo + down * sin_hi


def _norm_proj_kernel(*refs, has_bias, rot_cols):
    refs = list(refs)
    h_ref, gain_ref, shift_ref, scale_ref, w_ref = refs[:5]
    pos = 5
    bias_ref = None
    if has_bias:
        bias_ref = refs[pos]
        pos += 1
    if rot_cols:
        cos_ref, slo_ref, shi_ref = refs[pos:pos + 3]
        pos += 3
    o_ref, u_scr, inv_scr = refs[pos:pos + 3]

    @pl.when(pl.program_id(1) == 0)
    def _():
        _ada_norm_into(u_scr, h_ref, gain_ref, shift_ref, scale_ref, inv_scr)

    z = jnp.dot(u_scr[...], w_ref[...], preferred_element_type=F32)
    if has_bias:
        z = z + bias_ref[...]
    if rot_cols:
        cos, slo, shi = cos_ref[...], slo_ref[...], shi_ref[...]
        tn = z.shape[1]
        blocks = []
        for cb in range(tn // LANES):
            zb = z[:, cb * LANES:(cb + 1) * LANES]
            if cb * LANES < rot_cols:
                zb = _rotate_block(zb, cos, slo, shi)
            blocks.append(zb)
        z = jnp.concatenate(blocks, axis=1) if len(blocks) > 1 else blocks[0]
    o_ref[...] = z.astype(o_ref.dtype)


def _norm_proj(h, gain, shift, scale, w, seq, out_dtype, bias=None, rope=None, rot_cols=0, tm=1024, tn=None):
    T, D = h.shape
    N = w.shape[1]
    tn = N if tn is None else tn
    per_b = seq // tm
    vec = pl.BlockSpec((1, D), lambda i, j: (0, 0))
    bvec = pl.BlockSpec((None, 1, D), lambda i, j: (i // per_b, 0, 0))
    in_specs = [pl.BlockSpec((tm, D), lambda i, j: (i, 0)), vec, bvec, bvec,
                pl.BlockSpec((D, tn), lambda i, j: (0, j))]
    args = [h, gain, shift, scale, w]
    if bias is not None:
        in_specs.append(pl.BlockSpec((1, tn), lambda i, j: (0, j)))
        args.append(bias)
    if rot_cols:
        assert tn == N and rot_cols % LANES == 0
        tab = pl.BlockSpec((tm, LANES), lambda i, j: (i % per_b, 0))
        in_specs += [tab, tab, tab]
        args += list(rope)
    return pl.pallas_call(
        functools.partial(_norm_proj_kernel, has_bias=bias is not None, rot_cols=rot_cols),
        grid=(T // tm, N // tn),
        in_specs=in_specs,
        out_specs=pl.BlockSpec((tm, tn), lambda i, j: (i, j)),
        out_shape=jax.ShapeDtypeStruct((T, N), out_dtype),
        scratch_shapes=[pltpu.VMEM((tm, D), BF16), pltpu.VMEM((tm, LANES), F32)],
        compiler_params=_params("norm_proj", 2),
        name="norm_proj",
    )(*args)


def _proj_res_kernel(x_ref, w_ref, h_ref, gate_ref, o_ref):
    y = jnp.dot(x_ref[...], w_ref[...], preferred_element_type=F32)
    o_ref[...] = h_ref[...] + gate_ref[...] * y


def _proj_res(x, w, h, gate, seq, tm=512):
    T, K = x.shape
    N = w.shape[1]
    per_b = seq // tm
    return pl.pallas_call(
        _proj_res_kernel,
        grid=(T // tm,),
        in_specs=[
            pl.BlockSpec((tm, K), lambda i: (i, 0)),
            pl.BlockSpec((K, N), lambda i: (0, 0)),
            pl.BlockSpec((tm, N), lambda i: (i, 0)),
            pl.BlockSpec((None, 1, N), lambda i: (i // per_b, 0, 0)),
        ],
        out_specs=pl.BlockSpec((tm, N), lambda i: (i, 0)),
        out_shape=jax.ShapeDtypeStruct((T, N), F32),
        compiler_params=_params("proj_res", 1),
        name="proj_res",
    )(x, w, h, gate)


def _hgrn_scan_kernel(zq_ref, zf_ref, zi_ref, zg_ref, lbl_ref, hg_ref, o_ref,
                      st_ref, q_scr, b_scr, c_scr, v_scr, y_scr, bc_scr, cc_scr, mask_scr, spread_scr, tril_scr,
                      *, layer, n_chunks, n_heads):
    C = HGRN_CHUNK
    K = HGRN_HEAD
    levels = []
    m = C // 2
    while m >= SUBLANES:
        levels.append(m)
        m //= 2

    @pl.when(pl.program_id(2) == 0)
    def _():
        st_ref[...] = jnp.zeros_like(st_ref)
        row_c = lax.broadcasted_iota(jnp.int32, (C, C), 0)
        col_c = lax.broadcasted_iota(jnp.int32, (C, C), 1)
        tril = (row_c >= col_c).astype(BF16)
        tril_scr[...] = jnp.concatenate([tril, tril, tril], axis=1)
        xor_c = row_c ^ col_c
        mask_scr[0] = ((xor_c < SUBLANES) & (col_c <= row_c)).astype(F32)
        for li, m in enumerate(levels):
            if li:
                mask_scr[li] = (xor_c < 2 * m).astype(F32)
        src = lax.shift_right_logical(lax.broadcasted_iota(jnp.int32, (SUBLANES * K, C), 0), K.bit_length() - 1)
        dst = lax.broadcasted_iota(jnp.int32, (SUBLANES * K, C), 1) & (SUBLANES - 1)
        spread_scr[...] = (src == dst).astype(BF16)

    def prepare(hh, r0):
        lanes = slice(hh * K, (hh + 1) * K)
        rows = pl.ds(r0, C)
        logits = lbl_ref[:, lanes]
        e = jnp.exp(logits - jnp.max(logits, axis=0, keepdims=True))
        lb = jnp.sum(e[:layer + 1], axis=0, keepdims=True) / jnp.sum(e, axis=0, keepdims=True)

        zq = zq_ref[rows, lanes]
        q_scr[rows, lanes] = zq * _sigmoid(zq)
        forget = lb + (1.0 - lb) * _sigmoid(zf_ref[rows, lanes])
        v_scr[rows, lanes] = zi_ref[rows, lanes].astype(BF16)
        parts = jnp.concatenate(_split_bf16x3(jnp.log2(forget)), axis=0)
        b = jnp.dot(tril_scr[...], parts, preferred_element_type=F32)
        b_scr[rows, lanes] = b
        c_scr[rows, lanes] = b - jnp.log2(1.0 - forget)

    def mix(hh, r0):
        lanes = slice(hh * K, (hh + 1) * K)
        rows = pl.ds(r0, C)
        q = q_scr[rows, lanes]
        b = b_scr[rows, lanes]
        c = c_scr[rows, lanes]
        v16 = v_scr[rows, lanes]
        st = st_ref[hh]
        bc_scr[hh] = b
        cc_scr[hh] = c

        qd = (q * jnp.exp2(b)).astype(BF16)
        out = lax.dot_general(qd, st.astype(BF16), (((1,), (1,)), ((), ())), preferred_element_type=F32)
        b_last = bc_scr[hh, pl.ds(C - 1, 1), :]
        kd = jnp.exp2(b_last - c).astype(BF16)
        st_ref[hh] = st * jnp.exp2(b_last) + lax.dot_general(
            v16, kd, (((0,), (0,)), ((), ())), preferred_element_type=F32)

        w_rows = []
        for blk in range(C // SUBLANES):
            rb = blk * SUBLANES
            qb = q[rb:rb + SUBLANES]
            bb = b[rb:rb + SUBLANES]
            w_s = []
            for s in range(SUBLANES):
                rel = jnp.minimum(bb - cc_scr[hh, pl.ds(rb + s, 1), :], 0.0)
                w_s.append(qb * jnp.exp2(rel))
            w_rows.append(jnp.concatenate(w_s, axis=1))
        w_all = jnp.concatenate(w_rows, axis=0).astype(BF16)
        p = jnp.dot(w_all, spread_scr[...], preferred_element_type=F32) * mask_scr[0]

        n_groups = C // SUBLANES
        zero_group = jnp.zeros((SUBLANES, K), F32)
        for li, m in enumerate(levels):
            qt, kt = [], []
            for g in range(n_groups):
                rg = g * SUBLANES
                rows_g = slice(rg, rg + SUBLANES)
                anchor = (rg // (2 * m)) * 2 * m + m - 1
                ba = bc_scr[hh, pl.ds(anchor, 1), :]
                if rg & m:
                    qt.append(q[rows_g] * jnp.exp2(b[rows_g] - ba))
                    kt.append(zero_group)
                else:
                    qt.append(zero_group)
                    kt.append(jnp.exp2(ba - c[rows_g]))
            a_m = lax.dot_general(jnp.concatenate(qt, axis=0).astype(BF16),
                                  jnp.concatenate(kt, axis=0).astype(BF16),
                                  (((1,), (1,)), ((), ())), preferred_element_type=F32)
            p = p + (a_m if 2 * m == C else a_m * mask_scr[li])
        y_scr[rows, lanes] = out + jnp.dot(p.astype(BF16), v16, preferred_element_type=F32)

    def finish(hh, r0):
        lanes = slice(hh * K, (hh + 1) * K)
        rows = pl.ds(r0, C)
        out = y_scr[rows, lanes]
        y = out * lax.rsqrt(jnp.mean(out * out, axis=-1, keepdims=True) + NORM_EPS) * hg_ref[:, lanes]
        o_ref[rows, lanes] = (y * _sigmoid(zg_ref[rows, lanes])).astype(o_ref.dtype)

    def run(stage, ci):
        r0 = ci * C if isinstance(ci, int) else pl.multiple_of(ci * C, C)
        for hh in range(n_heads):
            stage(hh, r0)

    run(prepare, 0)
    run(mix, 0)
    run(prepare, 1)

    def trip(ci, carry):
        run(finish, ci - 1)
        run(mix, ci)
        run(prepare, jnp.minimum(ci + 1, n_chunks - 1))
        return carry
    lax.fori_loop(1, n_chunks, trip, 0)
    run(finish, n_chunks - 1)


def _hgrn_scan(z, lb_logits, head_gain, batch, seq, layer, ts=1024, hpb=4):
    assert HGRN_CHUNK == HGRN_HEAD
    T, D4 = z.shape
    D = D4 // 4
    n_s = seq // ts
    hw = hpb * HGRN_HEAD
    n_hb = D // hw

    def zspec(group):
        return pl.BlockSpec((ts, hw), lambda bi, hi, si: (bi * n_s + si, group * n_hb + hi))

    n_layers = lb_logits.shape[0]
    n_levels = (HGRN_CHUNK // SUBLANES).bit_length() - 1
    return pl.pallas_call(
        functools.partial(_hgrn_scan_kernel, layer=layer, n_chunks=ts // HGRN_CHUNK, n_heads=hpb),
        grid=(batch, n_hb, n_s),
        in_specs=[zspec(0), zspec(1), zspec(2), zspec(3),
                  pl.BlockSpec((n_layers, hw), lambda bi, hi, si: (0, hi)),
                  pl.BlockSpec((1, hw), lambda bi, hi, si: (0, hi))],
        out_specs=pl.BlockSpec((ts, hw), lambda bi, hi, si: (bi * n_s + si, hi)),
        out_shape=jax.ShapeDtypeStruct((T, D), BF16),
        scratch_shapes=[pltpu.VMEM((hpb, HGRN_HEAD, HGRN_HEAD), F32),
                        pltpu.VMEM((ts, hw), F32),
                        pltpu.VMEM((ts, hw), F32),
                        pltpu.VMEM((ts, hw), F32),
                        pltpu.VMEM((ts, hw), BF16),
                        pltpu.VMEM((ts, hw), F32),
                        pltpu.VMEM((hpb, HGRN_CHUNK, HGRN_HEAD), F32),
                        pltpu.VMEM((hpb, HGRN_CHUNK, HGRN_HEAD), F32),
                        pltpu.VMEM((n_levels, HGRN_CHUNK, HGRN_CHUNK), F32),
                        pltpu.VMEM((SUBLANES * HGRN_HEAD, HGRN_CHUNK), BF16),
                        pltpu.VMEM((HGRN_CHUNK, 3 * HGRN_CHUNK), BF16)],
        compiler_params=_params("hgrn_scan", 3),
        name="hgrn_scan",
    )(z, z, z, z, lb_logits, head_gain)


def _swa_kernel(sink_ref, q_ref, kc_ref, kp_ref, vc_ref, vp_ref, o_ref, bias_scr, *, n_blocks):
    W = WINDOW
    pair = pl.program_id(1)
    first_tile = pl.program_id(2) == 0
    lane_k = lax.broadcasted_iota(jnp.int32, (2 * W, LANES), 1)
    qi = lax.broadcasted_iota(jnp.int32, (W, 2 * W), 0)
    kj = lax.broadcasted_iota(jnp.int32, (W, 2 * W), 1)
    in_window = (kj > qi) & (kj <= qi + W)
    bias_scr[0] = jnp.where(in_window & (kj >= jnp.where(first_tile, W, 0)), 0.0, NEG_INF)
    bias_scr[1] = jnp.where(in_window, 0.0, NEG_INF)

    for n in range(n_blocks):
        rows = slice(n * W, (n + 1) * W)
        q2 = jnp.concatenate([q_ref[rows, j * LANES:(j + 1) * LANES] for j in range(GROUP)], axis=0)
        q2 = q2 * jnp.asarray(HEAD_DIM ** -0.5, q2.dtype)
        if n == 0:
            k_prev, v_prev = kp_ref[...], vp_ref[...]
        else:
            k_prev, v_prev = kc_ref[(n - 1) * W:n * W, :], vc_ref[(n - 1) * W:n * W, :]
        kk = jnp.concatenate([k_prev, kc_ref[rows, :]], axis=0)
        vv = jnp.concatenate([v_prev, vc_ref[rows, :]], axis=0)
        zero = jnp.zeros_like(kk)
        k_bd = jnp.concatenate([jnp.where(lane_k < HEAD_DIM, kk, zero),
                                jnp.where(lane_k >= HEAD_DIM, kk, zero)], axis=0)
        v_bd = jnp.concatenate([jnp.where(lane_k < HEAD_DIM, vv, zero),
                                jnp.where(lane_k >= HEAD_DIM, vv, zero)], axis=0)
        s = lax.dot_general(q2, k_bd, (((1,), (1,)), ((), ())), preferred_element_type=F32)
        bias = bias_scr[0 if n == 0 else 1]
        p_rows = []
        for j in range(GROUP):
            p_cols = []
            for gl in range(2):
                sub = s[j * W:(j + 1) * W, gl * 2 * W:(gl + 1) * 2 * W] + bias
                sink = sink_ref[(2 * pair + gl) * GROUP + j]
                mx = jnp.maximum(jnp.max(sub, axis=-1, keepdims=True), sink)
                p = jnp.exp(sub - mx)
                den = jnp.sum(p, axis=-1, keepdims=True) + jnp.exp(sink - mx)
                p_cols.append((p * (1.0 / den)).astype(BF16))
            p_rows.append(jnp.concatenate(p_cols, axis=1))
        probs = jnp.concatenate(p_rows, axis=0)
        out = jnp.dot(probs, v_bd, preferred_element_type=F32)
        for j in range(GROUP):
            o_ref[rows, j * LANES:(j + 1) * LANES] = out[j * W:(j + 1) * W].astype(o_ref.dtype)


def _swa(q, kv, sinks, batch, seq, tq=1024):
    T, DQ = q.shape
    n_pairs = kv.shape[1] // (2 * LANES)
    qw = DQ // n_pairs
    n_t = seq // tq
    wpt = tq // WINDOW

    def cur(col0):
        return pl.BlockSpec((tq, LANES), lambda bi, pi, ti: (bi * n_t + ti, col0 + pi))

    def prev(col0):
        return pl.BlockSpec(
            (WINDOW, LANES),
            lambda bi, pi, ti: (jnp.maximum((bi * n_t + ti) * wpt - 1, 0), col0 + pi))

    return pl.pallas_call(
        functools.partial(_swa_kernel, n_blocks=wpt),
        grid=(batch, n_pairs, n_t),
        in_specs=[pl.BlockSpec(memory_space=pltpu.SMEM),
                  pl.BlockSpec((tq, qw), lambda bi, pi, ti: (bi * n_t + ti, pi)),
                  cur(0), prev(0), cur(n_pairs), prev(n_pairs)],
        out_specs=pl.BlockSpec((tq, qw), lambda bi, pi, ti: (bi * n_t + ti, pi)),
        out_shape=jax.ShapeDtypeStruct((T, DQ), BF16),
        scratch_shapes=[pltpu.VMEM((2, WINDOW, 2 * WINDOW), F32)],
        compiler_params=_params("swa", 3),
        name="swa",
    )(sinks, q, kv, kv, kv, kv)


def _rope_lane_tables(seq):
    half = ROT_DIM // 2
    inv_freq = jnp.power(jnp.float32(ROPE_THETA), -jnp.arange(0, ROT_DIM, 2, dtype=F32) / ROT_DIM)
    ang = jnp.arange(seq, dtype=F32)[:, None] * inv_freq[None, :]
    sin, cos = jnp.sin(ang), jnp.cos(ang)
    ones = jnp.ones((seq, HEAD_DIM - ROT_DIM), F32)
    zeros_rest = jnp.zeros((seq, HEAD_DIM - ROT_DIM), F32)
    zeros_half = jnp.zeros((seq, half), F32)
    cos_h = jnp.concatenate([cos, cos, ones], axis=1)
    lo_h = jnp.concatenate([-sin, zeros_half, zeros_rest], axis=1)
    hi_h = jnp.concatenate([zeros_half, sin, zeros_rest], axis=1)
    reps = LANES // HEAD_DIM
    return tuple(jnp.tile(t, (1, reps)) for t in (cos_h, lo_h, hi_h))


def _pair_major_heads(n_q_heads):
    order = []
    for p in range(n_q_heads // (2 * GROUP)):
        for j in range(GROUP):
            for gl in range(2):
                order.append((2 * p + gl) * GROUP + j)
    return np.asarray(order, dtype=np.int32)


def kernel(x, c, norm_gain, w_ada, b_ada, w_ffn_in, w_ffn_out, w_hgrn_in, hgrn_lb_logits, hgrn_head_gain,
           w_hgrn_out, kv_gain, w_ada_kv, b_ada_kv, w_kv, b_kv, w_q, b_q, attn_sinks, w_attn_out, final_gain):
    B, S, D = x.shape
    T = B * S
    depth = w_ada.shape[0]
    n_a = w_hgrn_in.shape[0]
    d_ff = w_ffn_out.shape[2]
    n_q_heads = w_q.shape[2] // HEAD_DIM

    c_rows = jnp.pad(c, ((0, SUBLANES - B), (0, 0)))
    mod = _ada_mod(c_rows, w_ada, b_ada)[:, :B].reshape(depth, B, N_SUBLAYERS, 3, 1, D)
    mod_kv = _ada_mod(c_rows, w_ada_kv[None], b_ada_kv[None])[0, :B].reshape(B, 2, 1, D)

    heads = _pair_major_heads(n_q_heads)
    rope = _rope_lane_tables(S)

    def ffn_jobs(layer, k, rows=PACK_ROWS):
        return [(w_ffn_in, (layer, k), rows, d_ff, 0), (w_ffn_in, (layer, k), rows, d_ff, 1),
                (w_ffn_out, (layer, k), 2 * rows, D, 0)]

    def whole(w, lead=()):
        return (w, lead, PACK_ROWS, w.shape[-1], 0)

    def mixer_jobs(layer):
        if layer < n_a:
            return [whole(w_hgrn_in, (layer,)), whole(w_hgrn_out, (layer,))]
        bl = layer - n_a
        wq = w_q[bl].reshape(D, n_q_heads, HEAD_DIM)[:, heads].reshape(D, -1)
        wo = w_attn_out[bl].reshape(n_q_heads, HEAD_DIM, D)[heads].reshape(-1, D)
        return [whole(wq), whole(wo)]

    first = ffn_jobs(0, 0, rows=4 * PACK_ROWS)
    ffn_w = _cast_weights(first, max(job[0].shape[-2] // job[2] for job in first))

    h = x.reshape(T, D)
    kv = None
    for layer in range(depth):
        def vecs(sub):
            return (norm_gain[layer, sub][None], mod[layer, :, sub, 0], mod[layer, :, sub, 1], mod[layer, :, sub, 2])

        gain, shift, scale, gate = vecs(0)
        jobs = mixer_jobs(layer) + ffn_jobs(layer, 1)
        h, cast = _ffn(h, gain, shift, scale, gate, *ffn_w, S, cast_jobs=jobs)
        w_mix_in, w_mix_out, ffn_w = cast[0], cast[1], cast[2:]

        gain, shift, scale, gate = vecs(1)
        if layer < n_a:
            z = _norm_proj(h, gain, shift, scale, w_mix_in, S, F32, tn=1024)
            o = _hgrn_scan(z, hgrn_lb_logits, hgrn_head_gain[layer][None], B, S, layer)
        else:
            bl = layer - n_a
            bq = b_q[bl].reshape(n_q_heads, HEAD_DIM)[heads].reshape(1, -1)
            q = _norm_proj(h, gain, shift, scale, w_mix_in, S, BF16, bias=bq, rope=rope,
                           rot_cols=w_mix_in.shape[1], tm=512)
            o = _swa(q, kv, attn_sinks[bl], B, S)
        h = _proj_res(o, w_mix_out, h, gate, S)

        gain, shift, scale, gate = vecs(2)
        last = layer == depth - 1
        jobs = ([whole(w_kv)] if layer == n_a - 1 else []) + ([] if last else ffn_jobs(layer + 1, 0))
        h, cast = _ffn(h, gain, shift, scale, gate, *ffn_w, S, cast_jobs=jobs,
                       final_gain=final_gain[None] if last else None)

        if layer == n_a - 1:
            w_kv16, cast = cast[0], cast[1:]
            kv = _norm_proj(h, kv_gain[None], mod_kv[:, 0], mod_kv[:, 1], w_kv16, S, BF16,
                            bias=b_kv[None], rope=rope, rot_cols=w_kv.shape[1] // 2, tm=512)
        ffn_w = cast

    return h.reshape(B, S, D)
```

```python
import functools

import numpy as np
import jax
import jax.numpy as jnp
from jax import lax
from jax.experimental import pallas as pl
from jax.experimental.pallas import tpu as pltpu

F32 = jnp.float32
BF16 = jnp.bfloat16

NORM_EPS = 1e-6
NEG_INF = -1e30
ROPE_THETA = 500000.0

LANES = 128
SUBLANES = 8
MIB = 1 << 20

HGRN_HEAD = 128
HGRN_CHUNK = 128
HEAD_DIM = 64
ROT_DIM = HEAD_DIM // 4
GROUP = 8
WINDOW = 128
N_SUBLAYERS = 3


PACK_ROWS = 16

VMEM_MIB = dict(ada_mod=40, cast_weights=32, ffn=60, norm_proj=48, proj_res=48, hgrn_scan=32, swa=40)


def _params(name, n_grid_axes):
    return pltpu.CompilerParams(dimension_semantics=("arbitrary",) * n_grid_axes,
                                vmem_limit_bytes=VMEM_MIB[name] * MIB)


def _sigmoid(x):
    return 1.0 / (1.0 + jnp.exp(-x))


def _split_bf16x3(x):
    hi = x.astype(BF16)
    r1 = x - hi.astype(F32)
    mid = r1.astype(BF16)
    lo = (r1 - mid.astype(F32)).astype(BF16)
    return hi, mid, lo


def _partial_sums_to_inv_rms(inv_scr, d):
    ones = jnp.ones((LANES, LANES), BF16)
    ssq = None
    for part in _split_bf16x3(inv_scr[...]):
        s = jnp.dot(part, ones, preferred_element_type=F32)
        ssq = s if ssq is None else ssq + s
    inv_scr[...] = lax.rsqrt(ssq * (1.0 / d) + NORM_EPS)


def _ada_norm_into(u_ref, h_ref, gain_ref, shift_ref, scale_ref, inv_scr, zero_ref=None):
    n_rows, d = h_ref.shape
    n_cb = d // LANES

    def pass1(i, carry):
        r = pl.multiple_of(i * SUBLANES, SUBLANES)
        acc = None
        for cb in range(n_cb):
            xb = h_ref[pl.ds(r, SUBLANES), cb * LANES:(cb + 1) * LANES]
            acc = xb * xb if acc is None else acc + xb * xb
        inv_scr[pl.ds(r, SUBLANES), :] = acc
        return carry
    lax.fori_loop(0, n_rows // SUBLANES, pass1, 0, unroll=4)
    _partial_sums_to_inv_rms(inv_scr, d)

    mult = gain_ref[...] * (1.0 + scale_ref[...])
    shift = shift_ref[...]

    def pass2(i, carry):
        r = pl.multiple_of(i * PACK_ROWS, PACK_ROWS)
        rows = pl.ds(r, PACK_ROWS)
        inv = inv_scr[rows, :]
        for cb in range(n_cb):
            cols = slice(cb * LANES, (cb + 1) * LANES)
            u_ref[rows, cols] = ((h_ref[rows, cols] * inv) * mult[:, cols] + shift[:, cols]).astype(u_ref.dtype)
            if zero_ref is not None:
                zero_ref[rows, cols] = jnp.zeros((PACK_ROWS, LANES), zero_ref.dtype)
        return carry
    lax.fori_loop(0, n_rows // PACK_ROWS, pass2, 0, unroll=2)


def _mod_kernel(c_ref, w_ref, b_ref, o_ref):
    c = c_ref[...]
    cs = (c * _sigmoid(c)).astype(BF16)
    o_ref[...] = jnp.dot(cs, w_ref[...].astype(BF16), preferred_element_type=F32) + b_ref[...]


def _ada_mod(c_rows, w, b):
    L, D, N = w.shape
    tn = 1024
    return pl.pallas_call(
        _mod_kernel,
        grid=(L, N // tn),
        in_specs=[
            pl.BlockSpec((SUBLANES, D), lambda l, j: (0, 0)),
            pl.BlockSpec((None, D, tn), lambda l, j: (l, 0, j)),
            pl.BlockSpec((None, 1, tn), lambda l, j: (l, 0, j)),
        ],
        out_specs=pl.BlockSpec((None, SUBLANES, tn), lambda l, j: (l, 0, j)),
        out_shape=jax.ShapeDtypeStruct((L, SUBLANES, N), F32),
        compiler_params=_params("ada_mod", 2),
        name="ada_mod",
    )(c_rows, w, b.reshape(L, 1, N))


def _cast_specs(jobs, n_steps, step_of):
    in_specs, out_specs, out_shape = [], [], []
    for src, lead, block_rows, width, col_block in jobs:
        n_rows = src.shape[len(lead)]
        n_blocks = n_rows // block_rows
        assert n_rows % block_rows == 0 and n_blocks <= n_steps and block_rows % PACK_ROWS == 0

        def block_of(*idx, n_blocks=n_blocks):
            return jnp.minimum(step_of(*idx), n_blocks - 1)

        in_specs.append(pl.BlockSpec((None,) * len(lead) + (block_rows, width),
                                     lambda *idx, f=block_of, lead=lead, cb=col_block: lead + (f(*idx), cb)))
        out_specs.append(pl.BlockSpec((block_rows, width), lambda *idx, f=block_of: (f(*idx), 0)))
        out_shape.append(jax.ShapeDtypeStruct((n_rows, width), BF16))
    return in_specs, out_specs, out_shape


def _cast_kernel(*refs):
    n = len(refs) // 2
    for src_ref, dst_ref in zip(refs[:n], refs[n:]):
        dst_ref[...] = src_ref[...].astype(dst_ref.dtype)


def _cast_weights(jobs, n_steps):
    in_specs, out_specs, out_shape = _cast_specs(jobs, n_steps, lambda s: s)
    return pl.pallas_call(
        _cast_kernel,
        grid=(n_steps,),
        in_specs=in_specs,
        out_specs=out_specs,
        out_shape=out_shape,
        compiler_params=_params("cast_weights", 1),
        name="cast_weights",
    )(*[job[0] for job in jobs])


def _ffn_kernel(*refs, n_f, tail, final_norm, n_cast):
    refs = list(refs)
    h_ref, gain_ref, shift_ref, scale_ref, gate_ref, wa_ref, wb_ref, wo_ref = refs[:8]
    pos = 8
    fg_ref = None
    if final_norm:
        fg_ref = refs[pos]
        pos += 1
    cast_src = refs[pos:pos + n_cast]
    o_ref = refs[pos + n_cast]
    cast_dst = refs[pos + n_cast + 1:pos + 2 * n_cast + 1]
    u_scr, inv_scr = refs[pos + 2 * n_cast + 1:]
    j = pl.program_id(1)

    @pl.when(j == 0)
    def _():
        _ada_norm_into(u_scr, h_ref, gain_ref, shift_ref, scale_ref, inv_scr, zero_ref=o_ref)

    def accumulate(cols):
        u = u_scr[...]
        a = jnp.dot(u, wa_ref[:, :cols], preferred_element_type=F32)
        b = jnp.dot(u, wb_ref[:, :cols], preferred_element_type=F32)
        g = ((a * _sigmoid(a)) * b).astype(BF16)
        o_ref[...] += jnp.dot(g, wo_ref[:cols, :], preferred_element_type=F32)
        for src_ref, dst_ref in zip(cast_src, cast_dst):
            dst_ref[...] = src_ref[...].astype(dst_ref.dtype)

    tf = wa_ref.shape[1]
    if tail == tf:
        accumulate(tf)
    else:
        pl.when(j < n_f - 1)(lambda: accumulate(tf))
        pl.when(j == n_f - 1)(lambda: accumulate(tail))

    @pl.when(j == n_f - 1)
    def _():
        n_rows, d = o_ref.shape
        n_cb = d // LANES
        half_gate = 0.5 * gate_ref[...]

        def residual(i, carry):
            r = pl.multiple_of(i * SUBLANES, SUBLANES)
            rows = pl.ds(r, SUBLANES)
            acc = None
            for cb in range(n_cb):
                cols = slice(cb * LANES, (cb + 1) * LANES)
                out = h_ref[rows, cols] + half_gate[:, cols] * o_ref[rows, cols]
                o_ref[rows, cols] = out
                if final_norm:
                    acc = out * out if acc is None else acc + out * out
            if final_norm:
                inv_scr[rows, :] = acc
            return carry
        lax.fori_loop(0, n_rows // SUBLANES, residual, 0, unroll=4)

        if final_norm:
            _partial_sums_to_inv_rms(inv_scr, d)
            final_gain = fg_ref[...]

            def normalise(i, carry):
                r = pl.multiple_of(i * SUBLANES, SUBLANES)
                rows = pl.ds(r, SUBLANES)
                inv = inv_scr[rows, :]
                for cb in range(n_cb):
                    cols = slice(cb * LANES, (cb + 1) * LANES)
                    o_ref[rows, cols] = (o_ref[rows, cols] * inv) * final_gain[:, cols]
                return carry
            lax.fori_loop(0, n_rows // SUBLANES, normalise, 0, unroll=4)


def _ffn(h, gain, shift, scale, gate, w_a, w_b, w_out, seq, final_gain=None, cast_jobs=(), tm=512, tf=1024):
    T, D = h.shape
    d_ff = w_out.shape[0]
    n_f = pl.cdiv(d_ff, tf)
    tail = d_ff - (n_f - 1) * tf
    assert tail % LANES == 0
    per_b = seq // tm
    final_norm = final_gain is not None
    vec = pl.BlockSpec((1, D), lambda i, j: (0, 0))
    bvec = pl.BlockSpec((None, 1, D), lambda i, j: (i // per_b, 0, 0))
    in_specs = [
        pl.BlockSpec((tm, D), lambda i, j: (i, 0)),
        vec, bvec, bvec, bvec,
        pl.BlockSpec((D, tf), lambda i, j: (0, j)),
        pl.BlockSpec((D, tf), lambda i, j: (0, j)),
        pl.BlockSpec((tf, D), lambda i, j: (j, 0)),
    ]
    args = [h, gain, shift, scale, gate, w_a, w_b, w_out]
    if final_norm:
        in_specs.append(vec)
        args.append(final_gain)
    out_specs = [pl.BlockSpec((tm, D), lambda i, j: (i, 0))]
    out_shape = [jax.ShapeDtypeStruct((T, D), F32)]
    cast_in, cast_out, cast_shape = _cast_specs(cast_jobs, (T // tm) * n_f, lambda i, j: i * n_f + j)
    in_specs += cast_in
    args += [job[0] for job in cast_jobs]
    out_specs += cast_out
    out_shape += cast_shape
    outs = pl.pallas_call(
        functools.partial(_ffn_kernel, n_f=n_f, tail=tail, final_norm=final_norm, n_cast=len(cast_jobs)),
        grid=(T // tm, n_f),
        in_specs=in_specs,
        out_specs=out_specs,
        out_shape=out_shape,
        scratch_shapes=[pltpu.VMEM((tm, D), BF16), pltpu.VMEM((tm, LANES), F32)],
        compiler_params=_params("ffn", 2),
        name="ffn",
    )(*args)
    return outs[0], outs[1:]


def _rotate_block(z, cos, sin_lo, sin_hi):
    half = ROT_DIM // 2
    up = pltpu.roll(z, LANES - half, axis=1)
    down = pltpu.roll(z, half, axis=1)
    return z * cos + up * sin_lo + down * sin_hi


def _norm_proj_kernel(*refs, has_bias, rot_cols):
    refs = list(refs)
    h_ref, gain_ref, shift_ref, scale_ref, w_ref = refs[:5]
    pos = 5
    bias_ref = None
    if has_bias:
        bias_ref = refs[pos]
        pos += 1
    if rot_cols:
        cos_ref, slo_ref, shi_ref = refs[pos:pos + 3]
        pos += 3
    o_ref, u_scr, inv_scr = refs[pos:pos + 3]

    @pl.when(pl.program_id(1) == 0)
    def _():
        _ada_norm_into(u_scr, h_ref, gain_ref, shift_ref, scale_ref, inv_scr)

    z = jnp.dot(u_scr[...], w_ref[...], preferred_element_type=F32)
    if has_bias:
        z = z + bias_ref[...]
    if rot_cols:
        cos, slo, shi = cos_ref[...], slo_ref[...], shi_ref[...]
        tn = z.shape[1]
        blocks = []
        for cb in range(tn // LANES):
            zb = z[:, cb * LANES:(cb + 1) * LANES]
            if cb * LANES < rot_cols:
                zb = _rotate_block(zb, cos, slo, shi)
            blocks.append(zb)
        z = jnp.concatenate(blocks, axis=1) if len(blocks) > 1 else blocks[0]
    o_ref[...] = z.astype(o_ref.dtype)


def _norm_proj(h, gain, shift, scale, w, seq, out_dtype, bias=None, rope=None, rot_cols=0, tm=1024, tn=None):
    T, D = h.shape
    N = w.shape[1]
    tn = N if tn is None else tn
    per_b = seq // tm
    vec = pl.BlockSpec((1, D), lambda i, j: (0, 0))
    bvec = pl.BlockSpec((None, 1, D), lambda i, j: (i // per_b, 0, 0))
    in_specs = [pl.BlockSpec((tm, D), lambda i, j: (i, 0)), vec, bvec, bvec,
                pl.BlockSpec((D, tn), lambda i, j: (0, j))]
    args = [h, gain, shift, scale, w]
    if bias is not None:
        in_specs.append(pl.BlockSpec((1, tn), lambda i, j: (0, j)))
        args.append(bias)
    if rot_cols:
        assert tn == N and rot_cols % LANES == 0
        tab = pl.BlockSpec((tm, LANES), lambda i, j: (i % per_b, 0))
        in_specs += [tab, tab, tab]
        args += list(rope)
    return pl.pallas_call(
        functools.partial(_norm_proj_kernel, has_bias=bias is not None, rot_cols=rot_cols),
        grid=(T // tm, N // tn),
        in_specs=in_specs,
        out_specs=pl.BlockSpec((tm, tn), lambda i, j: (i, j)),
        out_shape=jax.ShapeDtypeStruct((T, N), out_dtype),
        scratch_shapes=[pltpu.VMEM((tm, D), BF16), pltpu.VMEM((tm, LANES), F32)],
        compiler_params=_params("norm_proj", 2),
        name="norm_proj",
    )(*args)


def _proj_res_kernel(x_ref, w_ref, h_ref, gate_ref, o_ref):
    y = jnp.dot(x_ref[...], w_ref[...], preferred_element_type=F32)
    o_ref[...] = h_ref[...] + gate_ref[...] * y


def _proj_res(x, w, h, gate, seq, tm=512):
    T, K = x.shape
    N = w.shape[1]
    per_b = seq // tm
    return pl.pallas_call(
        _proj_res_kernel,
        grid=(T // tm,),
        in_specs=[
            pl.BlockSpec((tm, K), lambda i: (i, 0)),
            pl.BlockSpec((K, N), lambda i: (0, 0)),
            pl.BlockSpec((tm, N), lambda i: (i, 0)),
            pl.BlockSpec((None, 1, N), lambda i: (i // per_b, 0, 0)),
        ],
        out_specs=pl.BlockSpec((tm, N), lambda i: (i, 0)),
        out_shape=jax.ShapeDtypeStruct((T, N), F32),
        compiler_params=_params("proj_res", 1),
        name="proj_res",
    )(x, w, h, gate)


def _hgrn_scan_kernel(zq_ref, zf_ref, zi_ref, zg_ref, lbl_ref, hg_ref, o_ref,
                      st_ref, q_scr, b_scr, c_scr, v_scr, y_scr, bc_scr, cc_scr, mask_scr, spread_scr, tril_scr,
                      *, layer, n_chunks, n_heads):
    C = HGRN_CHUNK
    K = HGRN_HEAD
    levels = []
    m = C // 2
    while m >= SUBLANES:
        levels.append(m)
        m //= 2

    @pl.when(pl.program_id(2) == 0)
    def _():
        st_ref[...] = jnp.zeros_like(st_ref)
        row_c = lax.broadcasted_iota(jnp.int32, (C, C), 0)
        col_c = lax.broadcasted_iota(jnp.int32, (C, C), 1)
        tril = (row_c >= col_c).astype(BF16)
        tril_scr[...] = jnp.concatenate([tril, tril, tril], axis=1)
        xor_c = row_c ^ col_c
        mask_scr[0] = ((xor_c < SUBLANES) & (col_c <= row_c)).astype(F32)
        for li, m in enumerate(levels):
            if li:
                mask_scr[li] = (xor_c < 2 * m).astype(F32)
        src = lax.shift_right_logical(lax.broadcasted_iota(jnp.int32, (SUBLANES * K, C), 0), K.bit_length() - 1)
        dst = lax.broadcasted_iota(jnp.int32, (SUBLANES * K, C), 1) & (SUBLANES - 1)
        spread_scr[...] = (src == dst).astype(BF16)

    def prepare(hh, r0):
        lanes = slice(hh * K, (hh + 1) * K)
        rows = pl.ds(r0, C)
        logits = lbl_ref[:, lanes]
        e = jnp.exp(logits - jnp.max(logits, axis=0, keepdims=True))
        lb = jnp.sum(e[:layer + 1], axis=0, keepdims=True) / jnp.sum(e, axis=0, keepdims=True)

        zq = zq_ref[rows, lanes]
        q_scr[rows, lanes] = zq * _sigmoid(zq)
        forget = lb + (1.0 - lb) * _sigmoid(zf_ref[rows, lanes])
        v_scr[rows, lanes] = zi_ref[rows, lanes].astype(BF16)
        parts = jnp.concatenate(_split_bf16x3(jnp.log2(forget)), axis=0)
        b = jnp.dot(tril_scr[...], parts, preferred_element_type=F32)
        b_scr[rows, lanes] = b
        c_scr[rows, lanes] = b - jnp.log2(1.0 - forget)

    def mix(hh, r0):
        lanes = slice(hh * K, (hh + 1) * K)
        rows = pl.ds(r0, C)
        q = q_scr[rows, lanes]
        b = b_scr[rows, lanes]
        c = c_scr[rows, lanes]
        v16 = v_scr[rows, lanes]
        st = st_ref[hh]
        bc_scr[hh] = b
        cc_scr[hh] = c

        qd = (q * jnp.exp2(b)).astype(BF16)
        out = lax.dot_general(qd, st.astype(BF16), (((1,), (1,)), ((), ())), preferred_element_type=F32)
        b_last = bc_scr[hh, pl.ds(C - 1, 1), :]
        kd = jnp.exp2(b_last - c).astype(BF16)
        st_ref[hh] = st * jnp.exp2(b_last) + lax.dot_general(
            v16, kd, (((0,), (0,)), ((), ())), preferred_element_type=F32)

        w_rows = []
        for blk in range(C // SUBLANES):
            rb = blk * SUBLANES
            qb = q[rb:rb + SUBLANES]
            bb = b[rb:rb + SUBLANES]
            w_s = []
            for s in range(SUBLANES):
                rel = jnp.minimum(bb - cc_scr[hh, pl.ds(rb + s, 1), :], 0.0)
                w_s.append(qb * jnp.exp2(rel))
            w_rows.append(jnp.concatenate(w_s, axis=1))
        w_all = jnp.concatenate(w_rows, axis=0).astype(BF16)
        p = jnp.dot(w_all, spread_scr[...], preferred_element_type=F32) * mask_scr[0]

        n_groups = C // SUBLANES
        zero_group = jnp.zeros((SUBLANES, K), F32)
        for li, m in enumerate(levels):
            qt, kt = [], []
            for g in range(n_groups):
                rg = g * SUBLANES
                rows_g = slice(rg, rg + SUBLANES)
                anchor = (rg // (2 * m)) * 2 * m + m - 1
                ba = bc_scr[hh, pl.ds(anchor, 1), :]
                if rg & m:
                    qt.append(q[rows_g] * jnp.exp2(b[rows_g] - ba))
                    kt.append(zero_group)
                else:
                    qt.append(zero_group)
                    kt.append(jnp.exp2(ba - c[rows_g]))
            a_m = lax.dot_general(jnp.concatenate(qt, axis=0).astype(BF16),
                                  jnp.concatenate(kt, axis=0).astype(BF16),
                                  (((1,), (1,)), ((), ())), preferred_element_type=F32)
            p = p + (a_m if 2 * m == C else a_m * mask_scr[li])
        y_scr[rows, lanes] = out + jnp.dot(p.astype(BF16), v16, preferred_element_type=F32)

    def finish(hh, r0):
        lanes = slice(hh * K, (hh + 1) * K)
        rows = pl.ds(r0, C)
        out = y_scr[rows, lanes]
        y = out * lax.rsqrt(jnp.mean(out * out, axis=-1, keepdims=True) + NORM_EPS) * hg_ref[:, lanes]
        o_ref[rows, lanes] = (y * _sigmoid(zg_ref[rows, lanes])).astype(o_ref.dtype)

    def run(stage, ci):
        r0 = ci * C if isinstance(ci, int) else pl.multiple_of(ci * C, C)
        for hh in range(n_heads):
            stage(hh, r0)

    run(prepare, 0)
    run(mix, 0)
    run(prepare, 1)

    def trip(ci, carry):
        run(finish, ci - 1)
        run(mix, ci)
        run(prepare, jnp.minimum(ci + 1, n_chunks - 1))
        return carry
    lax.fori_loop(1, n_chunks, trip, 0)
    run(finish, n_chunks - 1)


def _hgrn_scan(z, lb_logits, head_gain, batch, seq, layer, ts=1024, hpb=4):
    assert HGRN_CHUNK == HGRN_HEAD
    T, D4 = z.shape
    D = D4 // 4
    n_s = seq // ts
    hw = hpb * HGRN_HEAD
    n_hb = D // hw

    def zspec(group):
        return pl.BlockSpec((ts, hw), lambda bi, hi, si: (bi * n_s + si, group * n_hb + hi))

    n_layers = lb_logits.shape[0]
    n_levels = (HGRN_CHUNK // SUBLANES).bit_length() - 1
    return pl.pallas_call(
        functools.partial(_hgrn_scan_kernel, layer=layer, n_chunks=ts // HGRN_CHUNK, n_heads=hpb),
        grid=(batch, n_hb, n_s),
        in_specs=[zspec(0), zspec(1), zspec(2), zspec(3),
                  pl.BlockSpec((n_layers, hw), lambda bi, hi, si: (0, hi)),
                  pl.BlockSpec((1, hw), lambda bi, hi, si: (0, hi))],
        out_specs=pl.BlockSpec((ts, hw), lambda bi, hi, si: (bi * n_s + si, hi)),
        out_shape=jax.ShapeDtypeStruct((T, D), BF16),
        scratch_shapes=[pltpu.VMEM((hpb, HGRN_HEAD, HGRN_HEAD), F32),
                        pltpu.VMEM((ts, hw), F32),
                        pltpu.VMEM((ts, hw), F32),
                        pltpu.VMEM((ts, hw), F32),
                        pltpu.VMEM((ts, hw), BF16),
                        pltpu.VMEM((ts, hw), F32),
                        pltpu.VMEM((hpb, HGRN_CHUNK, HGRN_HEAD), F32),
                        pltpu.VMEM((hpb, HGRN_CHUNK, HGRN_HEAD), F32),
                        pltpu.VMEM((n_levels, HGRN_CHUNK, HGRN_CHUNK), F32),
                        pltpu.VMEM((SUBLANES * HGRN_HEAD, HGRN_CHUNK), BF16),
                        pltpu.VMEM((HGRN_CHUNK, 3 * HGRN_CHUNK), BF16)],
        compiler_params=_params("hgrn_scan", 3),
        name="hgrn_scan",
    )(z, z, z, z, lb_logits, head_gain)


def _swa_kernel(sink_ref, q_ref, kc_ref, kp_ref, vc_ref, vp_ref, o_ref, bias_scr, *, n_blocks):
    W = WINDOW
    pair = pl.program_id(1)
    first_tile = pl.program_id(2) == 0
    lane_k = lax.broadcasted_iota(jnp.int32, (2 * W, LANES), 1)
    qi = lax.broadcasted_iota(jnp.int32, (W, 2 * W), 0)
    kj = lax.broadcasted_iota(jnp.int32, (W, 2 * W), 1)
    in_window = (kj > qi) & (kj <= qi + W)
    bias_scr[0] = jnp.where(in_window & (kj >= jnp.where(first_tile, W, 0)), 0.0, NEG_INF)
    bias_scr[1] = jnp.where(in_window, 0.0, NEG_INF)

    for n in range(n_blocks):
        rows = slice(n * W, (n + 1) * W)
        q2 = jnp.concatenate([q_ref[rows, j * LANES:(j + 1) * LANES] for j in range(GROUP)], axis=0)
        q2 = q2 * jnp.asarray(HEAD_DIM ** -0.5, q2.dtype)
        if n == 0:
            k_prev, v_prev = kp_ref[...], vp_ref[...]
        else:
            k_prev, v_prev = kc_ref[(n - 1) * W:n * W, :], vc_ref[(n - 1) * W:n * W, :]
        kk = jnp.concatenate([k_prev, kc_ref[rows, :]], axis=0)
        vv = jnp.concatenate([v_prev, vc_ref[rows, :]], axis=0)
        zero = jnp.zeros_like(kk)
        k_bd = jnp.concatenate([jnp.where(lane_k < HEAD_DIM, kk, zero),
                                jnp.where(lane_k >= HEAD_DIM, kk, zero)], axis=0)
        v_bd = jnp.concatenate([jnp.where(lane_k < HEAD_DIM, vv, zero),
                                jnp.where(lane_k >= HEAD_DIM, vv, zero)], axis=0)
        s = lax.dot_general(q2, k_bd, (((1,), (1,)), ((), ())), preferred_element_type=F32)
        bias = bias_scr[0 if n == 0 else 1]
        p_rows = []
        for j in range(GROUP):
            p_cols = []
            for gl in range(2):
                sub = s[j * W:(j + 1) * W, gl * 2 * W:(gl + 1) * 2 * W] + bias
                sink = sink_ref[(2 * pair + gl) * GROUP + j]
                mx = jnp.maximum(jnp.max(sub, axis=-1, keepdims=True), sink)
                p = jnp.exp(sub - mx)
                den = jnp.sum(p, axis=-1, keepdims=True) + jnp.exp(sink - mx)
                p_cols.append((p * (1.0 / den)).astype(BF16))
            p_rows.append(jnp.concatenate(p_cols, axis=1))
        probs = jnp.concatenate(p_rows, axis=0)
        out = jnp.dot(probs, v_bd, preferred_element_type=F32)
        for j in range(GROUP):
            o_ref[rows, j * LANES:(j + 1) * LANES] = out[j * W:(j + 1) * W].astype(o_ref.dtype)


def _swa(q, kv, sinks, batch, seq, tq=1024):
    T, DQ = q.shape
    n_pairs = kv.shape[1] // (2 * LANES)
    qw = DQ // n_pairs
    n_t = seq // tq
    wpt = tq // WINDOW

    def cur(col0):
        return pl.BlockSpec((tq, LANES), lambda bi, pi, ti: (bi * n_t + ti, col0 + pi))

    def prev(col0):
        return pl.BlockSpec(
            (WINDOW, LANES),
            lambda bi, pi, ti: (jnp.maximum((bi * n_t + ti) * wpt - 1, 0), col0 + pi))

    return pl.pallas_call(
        functools.partial(_swa_kernel, n_blocks=wpt),
        grid=(batch, n_pairs, n_t),
        in_specs=[pl.BlockSpec(memory_space=pltpu.SMEM),
                  pl.BlockSpec((tq, qw), lambda bi, pi, ti: (bi * n_t + ti, pi)),
                  cur(0), prev(0), cur(n_pairs), prev(n_pairs)],
        out_specs=pl.BlockSpec((tq, qw), lambda bi, pi, ti: (bi * n_t + ti, pi)),
        out_shape=jax.ShapeDtypeStruct((T, DQ), BF16),
        scratch_shapes=[pltpu.VMEM((2, WINDOW, 2 * WINDOW), F32)],
        compiler_params=_params("swa", 3),
        name="swa",
    )(sinks, q, kv, kv, kv, kv)


def _rope_lane_tables(seq):
    half = ROT_DIM // 2
    inv_freq = jnp.power(jnp.float32(ROPE_THETA), -jnp.arange(0, ROT_DIM, 2, dtype=F32) / ROT_DIM)
    ang = jnp.arange(seq, dtype=F32)[:, None] * inv_freq[None, :]
    sin, cos = jnp.sin(ang), jnp.cos(ang)
    ones = jnp.ones((seq, HEAD_DIM - ROT_DIM), F32)
    zeros_rest = jnp.zeros((seq, HEAD_DIM - ROT_DIM), F32)
    zeros_half = jnp.zeros((seq, half), F32)
    cos_h = jnp.concatenate([cos, cos, ones], axis=1)
    lo_h = jnp.concatenate([-sin, zeros_half, zeros_rest], axis=1)
    hi_h = jnp.concatenate([zeros_half, sin, zeros_rest], axis=1)
    reps = LANES // HEAD_DIM
    return tuple(jnp.tile(t, (1, reps)) for t in (cos_h, lo_h, hi_h))


def _pair_major_heads(n_q_heads):
    order = []
    for p in range(n_q_heads // (2 * GROUP)):
        for j in range(GROUP):
            for gl in range(2):
                order.append((2 * p + gl) * GROUP + j)
    return np.asarray(order, dtype=np.int32)


def kernel(x, c, norm_gain, w_ada, b_ada, w_ffn_in, w_ffn_out, w_hgrn_in, hgrn_lb_logits, hgrn_head_gain,
           w_hgrn_out, kv_gain, w_ada_kv, b_ada_kv, w_kv, b_kv, w_q, b_q, attn_sinks, w_attn_out, final_gain):
    B, S, D = x.shape
    T = B * S
    depth = w_ada.shape[0]
    n_a = w_hgrn_in.shape[0]
    d_ff = w_ffn_out.shape[2]
    n_q_heads = w_q.shape[2] // HEAD_DIM

    c_rows = jnp.pad(c, ((0, SUBLANES - B), (0, 0)))
    mod = _ada_mod(c_rows, w_ada, b_ada)[:, :B].reshape(depth, B, N_SUBLAYERS, 3, 1, D)
    mod_kv = _ada_mod(c_rows, w_ada_kv[None], b_ada_kv[None])[0, :B].reshape(B, 2, 1, D)

    heads = _pair_major_heads(n_q_heads)
    rope = _rope_lane_tables(S)

    def ffn_jobs(layer, k, rows=PACK_ROWS):
        return [(w_ffn_in, (layer, k), rows, d_ff, 0), (w_ffn_in, (layer, k), rows, d_ff, 1),
                (w_ffn_out, (layer, k), 2 * rows, D, 0)]

    def whole(w, lead=()):
        return (w, lead, PACK_ROWS, w.shape[-1], 0)

    def mixer_jobs(layer):
        if layer < n_a:
            return [whole(w_hgrn_in, (layer,)), whole(w_hgrn_out, (layer,))]
        bl = layer - n_a
        wq = w_q[bl].reshape(D, n_q_heads, HEAD_DIM)[:, heads].reshape(D, -1)
        wo = w_attn_out[bl].reshape(n_q_heads, HEAD_DIM, D)[heads].reshape(-1, D)
        return [whole(wq), whole(wo)]

    first = ffn_jobs(0, 0, rows=4 * PACK_ROWS)
    ffn_w = _cast_weights(first, max(job[0].shape[-2] // job[2] for job in first))

    h = x.reshape(T, D)
    kv = None
    for layer in range(depth):
        def vecs(sub):
            return (norm_gain[layer, sub][None], mod[layer, :, sub, 0], mod[layer, :, sub, 1], mod[layer, :, sub, 2])

        gain, shift, scale, gate = vecs(0)
        jobs = mixer_jobs(layer) + ffn_jobs(layer, 1)
        h, cast = _ffn(h, gain, shift, scale, gate, *ffn_w, S, cast_jobs=jobs)
        w_mix_in, w_mix_out, ffn_w = cast[0], cast[1], cast[2:]

        gain, shift, scale, gate = vecs(1)
        if layer < n_a:
            z = _norm_proj(h, gain, shift, scale, w_mix_in, S, F32, tn=1024)
            o = _hgrn_scan(z, hgrn_lb_logits, hgrn_head_gain[layer][None], B, S, layer)
        else:
            bl = layer - n_a
            bq = b_q[bl].reshape(n_q_heads, HEAD_DIM)[heads].reshape(1, -1)
            q = _norm_proj(h, gain, shift, scale, w_mix_in, S, BF16, bias=bq, rope=rope,
                           rot_cols=w_mix_in.shape[1], tm=512)
            o = _swa(q, kv, attn_sinks[bl], B, S)
        h = _proj_res(o, w_mix_out, h, gate, S)

        gain, shift, scale, gate = vecs(2)
        last = layer == depth - 1
        jobs = ([whole(w_kv)] if layer == n_a - 1 else []) + ([] if last else ffn_jobs(layer + 1, 0))
        h, cast = _ffn(h, gain, shift, scale, gate, *ffn_w, S, cast_jobs=jobs,
                       final_gain=final_gain[None] if last else None)

        if layer == n_a - 1:
            w_kv16, cast = cast[0], cast[1:]
            kv = _norm_proj(h, kv_gain[None], mod_kv[:, 0], mod_kv[:, 1], w_kv16, S, BF16,
                            bias=b_kv[None], rope=rope, rot_cols=w_kv.shape[1] // 2, tm=512)
        ffn_w = cast

    return h.reshape(B, S, D)
```
